```python
import math
import jax
import jax.numpy as jnp
from jax import lax
import numpy as np

D_MODEL = 2048
BATCH = 4
SEQ = 8192
DEPTH = 2
DEC_BATCH = 2
DEC_SEQ = 8192
PAST_LEN = 128

HEAD_DIM = 64
ATTN_WIDTH = D_MODEL // 2
CONV_WIDTH = D_MODEL - ATTN_WIDTH
N_ATTN_HEADS = ATTN_WIDTH // HEAD_DIM
IN_COLS = 3 * ATTN_WIDTH + 2 * CONV_WIDTH
DILATED_BRANCHES = ((128, 1), (512, 4), (2048, 16))
BLOCK = 64
CONV_KERNEL = 31
CONV_PAD = (CONV_KERNEL - 1) // 2
D_FF = 4 * D_MODEL
N_MOD = 6
DEEPNORM_ALPHA = (2 * DEPTH) ** 0.25
DEEPNORM_BETA = (8 * DEPTH) ** -0.25
LN_EPS = 1e-5
NEG_INF = -1e30

kernel_name = "hybrid_dilated_attn_conformer_encoder"


def layer_norm(x, g, b):
    xf = x.astype(jnp.float32)
    mu = jnp.mean(xf, axis=-1, keepdims=True)
    var = jnp.mean(jnp.square(xf - mu), axis=-1, keepdims=True)
    y = (xf - mu) * lax.rsqrt(var + LN_EPS) * g.astype(jnp.float32) + b.astype(jnp.float32)
    return y.astype(x.dtype)


def rms_norm(x, g):
    xf = x.astype(jnp.float32)
    y = xf * lax.rsqrt(jnp.mean(jnp.square(xf), axis=-1, keepdims=True) + LN_EPS) * g.astype(jnp.float32)
    return y.astype(x.dtype)


def alibi_slopes(n_heads):
    return 2.0 ** (-8.0 * jnp.arange(1, n_heads + 1, dtype=jnp.float32) / n_heads)


def dilated_window_branch(q, k, v, window, dilation, slopes):
    B, S, H, Dh = q.shape
    L = S // dilation
    half = window // (2 * dilation)
    assert half <= BLOCK
    N = B * dilation
    n_blk = -(-L // BLOCK)
    Lp = n_blk * BLOCK

    def to_strided(t):
        return t.reshape(B, L, dilation, H, Dh).transpose(0, 2, 1, 3, 4).reshape(N, L, H, Dh)

    qs, ks, vs = to_strided(q), to_strided(k), to_strided(v)
    qb = jnp.pad(qs, ((0, 0), (0, Lp - L), (0, 0), (0, 0))).reshape(N, n_blk, BLOCK, H, Dh)

    def neighbourhood(t):
        tb = jnp.pad(t, ((0, 0), (BLOCK, Lp - L + BLOCK), (0, 0), (0, 0))).reshape(N, n_blk + 2, BLOCK, H, Dh)
        return jnp.concatenate([tb[:, :-2], tb[:, 1:-1], tb[:, 2:]], axis=2)

    kb, vb = neighbourhood(ks), neighbourhood(vs)
    q_pos = jnp.arange(Lp).reshape(n_blk, BLOCK)
    k_pos = (jnp.arange(n_blk)[:, None] - 1) * BLOCK + jnp.arange(3 * BLOCK)[None, :]
    rel = k_pos[:, None, :] - q_pos[:, :, None]
    valid = (jnp.abs(rel) <= half) & (k_pos[:, None, :] >= 0) & (k_pos[:, None, :] < L)
    dist = (dilation * jnp.abs(rel)).astype(jnp.float32)
    bias = -slopes[None, :, None, None] * dist[:, None, :, :]

    scores = jnp.einsum("nbqhd,nbkhd->nbhqk", qb.astype(jnp.float32), kb.astype(jnp.float32)) * (Dh ** -0.5)
    scores = jnp.where(valid[:, None, :, :], scores + bias, NEG_INF)
    lse = jax.nn.logsumexp(scores, axis=-1)
    p = jnp.exp(scores - lse[..., None])
    o = jnp.einsum("nbhqk,nbkhd->nbqhd", p, vb.astype(jnp.float32))
    o = o.reshape(N, Lp, H, Dh)[:, :L]
    lse = lse.transpose(0, 1, 3, 2).reshape(N, Lp, H)[:, :L]
    o = o.reshape(B, dilation, L, H, Dh).transpose(0, 2, 1, 3, 4).reshape(B, S, H, Dh)
    lse = lse.reshape(B, dilation, L, H).transpose(0, 2, 1, 3).reshape(B, S, H)
    return o, lse


def dilated_mixture_attention(q, k, v):
    slopes = alibi_slopes(q.shape[2])
    outs, lses = [], []
    for window, dilation in DILATED_BRANCHES:
        o, lse = dilated_window_branch(q, k, v, window, dilation, slopes)
        outs.append(o)
        lses.append(lse)
    w = jax.nn.softmax(jnp.stack(lses), axis=0)
    return jnp.einsum("rbsh,rbshd->bshd", w, jnp.stack(outs))


def hybrid_mixer(h, w_in, conv_w, conv_b, conv_ln_g, conv_ln_b, attn_out_g, conv_out_g, w_out):
    B, S, _ = h.shape
    proj = h @ w_in
    q, k, v, u_val, u_gate = jnp.split(
        proj, [ATTN_WIDTH, 2 * ATTN_WIDTH, 3 * ATTN_WIDTH, 3 * ATTN_WIDTH + CONV_WIDTH], axis=-1)
    heads = (B, S, N_ATTN_HEADS, HEAD_DIM)
    attn = dilated_mixture_attention(q.reshape(heads), k.reshape(heads), v.reshape(heads))
    attn = attn.reshape(B, S, ATTN_WIDTH).astype(h.dtype)
    u = u_val * jax.nn.sigmoid(u_gate)
    u = lax.conv_general_dilated(
        u, conv_w[:, None, :], window_strides=(1,), padding=[(CONV_PAD, CONV_PAD)],
        dimension_numbers=("NWC", "WIO", "NWC"), feature_group_count=CONV_WIDTH) + conv_b
    u = jax.nn.silu(layer_norm(u, conv_ln_g, conv_ln_b))
    merged = jnp.concatenate([rms_norm(attn, attn_out_g), rms_norm(u, conv_out_g)], axis=-1)
    return merged @ w_out


def squared_relu_mlp(h, w_mlp1, b_mlp1, w_mlp2, b_mlp2):
    return jnp.square(jax.nn.relu(h @ w_mlp1 + b_mlp1)) @ w_mlp2 + b_mlp2


def encoder_trunk(x, c, emb_ln_g, emb_ln_b, w_ada, b_ada, w_in, conv_w, conv_b, conv_ln_g, conv_ln_b,
                  attn_out_g, conv_out_g, w_out, ln1_g, ln1_b, w_mlp1, b_mlp1, w_mlp2, b_mlp2, ln2_g, ln2_b):
    x = layer_norm(x, emb_ln_g, emb_ln_b)
    c_act = jax.nn.silu(c)
    for l in range(DEPTH):
        mod = (c_act @ w_ada[l] + b_ada[l])[:, None, :]
        sh1, sc1, g1, sh2, sc2, g2 = jnp.split(mod, N_MOD, axis=-1)
        h = x * (1 + sc1) + sh1
        mix = hybrid_mixer(h, w_in[l], conv_w[l], conv_b[l], conv_ln_g[l], conv_ln_b[l],
                           attn_out_g[l], conv_out_g[l], w_out[l])
        x = layer_norm(DEEPNORM_ALPHA * x + (1 + g1) * mix, ln1_g[l], ln1_b[l])
        h = x * (1 + sc2) + sh2
        ff = squared_relu_mlp(h, w_mlp1[l], b_mlp1[l], w_mlp2[l], b_mlp2[l])
        x = layer_norm(DEEPNORM_ALPHA * x + (1 + g2) * ff, ln2_g[l], ln2_b[l])
    return x


def setup_inputs(seed: int = 0) -> dict:
    key = jax.random.key(seed)
    ks = jax.random.split(key, 24)
    f32 = jnp.float32

    def nrm(k, shape, scale):
        return jax.random.normal(k, shape, f32) * scale

    def gain(k, shape):
        return 1.0 + nrm(k, shape, 0.02)

    col_scale = jnp.concatenate([
        jnp.ones((2 * ATTN_WIDTH,), f32),
        jnp.full((ATTN_WIDTH,), DEEPNORM_BETA, f32),
        jnp.ones((2 * CONV_WIDTH,), f32)])
    return {
        "x_prompt": nrm(ks[0], (BATCH, SEQ, D_MODEL), 1.0),
        "x_sample": nrm(ks[1], (DEC_BATCH, DEC_SEQ, D_MODEL), 1.0),
        "c_prompt": nrm(ks[2], (BATCH, D_MODEL), 1.0),
        "c_sample": nrm(ks[3], (DEC_BATCH, D_MODEL), 1.0),
        "emb_ln_g": gain(ks[4], (D_MODEL,)),
        "emb_ln_b": nrm(ks[5], (D_MODEL,), 0.02),
        "w_ada": nrm(ks[6], (DEPTH, D_MODEL, N_MOD * D_MODEL), 0.2 * D_MODEL ** -0.5),
        "b_ada": nrm(ks[7], (DEPTH, N_MOD * D_MODEL), 0.02),
        "w_in": nrm(ks[8], (DEPTH, D_MODEL, IN_COLS), D_MODEL ** -0.5) * col_scale,
        "conv_w": nrm(ks[9], (DEPTH, CONV_KERNEL, CONV_WIDTH), CONV_KERNEL ** -0.5),
        "conv_b": nrm(ks[10], (DEPTH, CONV_WIDTH), 0.02),
        "conv_ln_g": gain(ks[11], (DEPTH, CONV_WIDTH)),
        "conv_ln_b": nrm(ks[12], (DEPTH, CONV_WIDTH), 0.02),
        "attn_out_g": gain(ks[13], (DEPTH, ATTN_WIDTH)),
        "conv_out_g": gain(ks[14], (DEPTH, CONV_WIDTH)),
        "w_out": nrm(ks[15], (DEPTH, D_MODEL, D_MODEL), DEEPNORM_BETA * D_MODEL ** -0.5),
        "ln1_g": gain(ks[16], (DEPTH, D_MODEL)),
        "ln1_b": nrm(ks[17], (DEPTH, D_MODEL), 0.02),
        "w_mlp1": nrm(ks[18], (DEPTH, D_MODEL, D_FF), D_MODEL ** -0.5),
        "b_mlp1": nrm(ks[19], (DEPTH, D_FF), 0.02),
        "w_mlp2": nrm(ks[20], (DEPTH, D_FF, D_MODEL), DEEPNORM_BETA * D_FF ** -0.5),
        "b_mlp2": nrm(ks[21], (DEPTH, D_MODEL), 0.02),
        "ln2_g": gain(ks[22], (DEPTH, D_MODEL)),
        "ln2_b": nrm(ks[23], (DEPTH, D_MODEL), 0.02),
    }


def reference(x_prompt, x_sample, c_prompt, c_sample, emb_ln_g, emb_ln_b, w_ada, b_ada, w_in, conv_w, conv_b,
              conv_ln_g, conv_ln_b, attn_out_g, conv_out_g, w_out, ln1_g, ln1_b, w_mlp1, b_mlp1, w_mlp2, b_mlp2,
              ln2_g, ln2_b):
    weights = (emb_ln_g, emb_ln_b, w_ada, b_ada, w_in, conv_w, conv_b, conv_ln_g, conv_ln_b,
               attn_out_g, conv_out_g, w_out, ln1_g, ln1_b, w_mlp1, b_mlp1, w_mlp2, b_mlp2, ln2_g, ln2_b)
    y_prompt = encoder_trunk(x_prompt, c_prompt, *weights)
    y_sample = encoder_trunk(x_sample, c_sample, *weights)
    return (y_prompt, y_sample)
```

```python
import functools

import jax
import jax.numpy as jnp
from jax import lax
from jax.experimental import pallas as pl
from jax.experimental.pallas import tpu as pltpu

F32 = jnp.float32
BF16 = jnp.bfloat16

HEAD_DIM = 64
DILATED_BRANCHES = ((128, 1), (512, 4), (2048, 16))
HALF_WINDOW = 64
CONV_KERNEL = 31
CONV_PAD = (CONV_KERNEL - 1) // 2
N_MOD = 6
LN_EPS = 1e-5
NEG_INF = -1e30

LANES = 128
VMEM_BYTES_V7X = 64 * 1024 * 1024
VMEM_CAP = VMEM_BYTES_V7X - 8 * 1024 * 1024

Q_BLOCK = 128
K_WINDOW = Q_BLOCK + 2 * HALF_WINDOW


def _nbytes(shape, dtype):
    n = 1
    for s in shape:
        n *= s
    return n * jnp.dtype(dtype).itemsize


def _vmem_limit(pipelined, scratch=0, temps=0):
    return int(min(VMEM_CAP, 2 * pipelined + scratch + temps + (2 << 20)))


def _tile(n, preferred, align=LANES):
    t = min(preferred, n) // align * align
    while n % t:
        t -= align
    return t


def _params(semantics, vmem):
    return pltpu.CompilerParams(dimension_semantics=semantics, vmem_limit_bytes=vmem)


def _layer_norm(y, g, b):
    mu = jnp.mean(y, axis=-1, keepdims=True)
    yc = y - mu
    var = jnp.mean(yc * yc, axis=-1, keepdims=True)
    return yc * lax.rsqrt(var + LN_EPS) * g + b


def _rms_norm(y, g):
    return y * lax.rsqrt(jnp.mean(y * y, axis=-1, keepdims=True) + LN_EPS) * g


def _dot(a, b):
    return jnp.dot(a, b, preferred_element_type=F32)


def _ada_kernel(c_ref, w_ref, b_ref, o_ref):
    c = c_ref[...]
    a = c * jax.nn.sigmoid(c)
    w = w_ref[0]
    a_hi = a.astype(BF16)
    a_lo = (a - a_hi.astype(F32)).astype(BF16)
    w_hi = w.astype(BF16)
    w_lo = (w - w_hi.astype(F32)).astype(BF16)
    o_ref[0] = _dot(a_hi, w_hi) + _dot(a_hi, w_lo) + _dot(a_lo, w_hi) + b_ref[0]


def _ada_mod(c, w_ada, b_ada):
    depth, d, n = w_ada.shape
    rows = c.shape[0]
    tn = _tile(n, 1024)
    blocks = _nbytes((rows, d), F32) + _nbytes((d, tn), F32) + 2 * _nbytes((8, tn), F32)
    return pl.pallas_call(
        _ada_kernel,
        grid=(depth, n // tn),
        in_specs=[
            pl.BlockSpec((rows, d), lambda l, j: (0, 0)),
            pl.BlockSpec((1, d, tn), lambda l, j: (l, 0, j)),
            pl.BlockSpec((1, 1, tn), lambda l, j: (l, 0, j)),
        ],
        out_specs=pl.BlockSpec((1, rows, tn), lambda l, j: (l, 0, j)),
        out_shape=jax.ShapeDtypeStruct((depth, rows, n), F32),
        compiler_params=_params(("parallel", "parallel"), _vmem_limit(blocks, temps=2 * _nbytes((d, tn), F32))),
        name="ada_mod",
    )(c, w_ada, b_ada.reshape(depth, 1, n))


def _embed_kernel(x_ref, g_ref, b_ref, sc_ref, sh_ref, x_out, h_out):
    x = _layer_norm(x_ref[0], g_ref[...], b_ref[...])
    x_out[0] = x
    h_out[0] = (x * (1.0 + sc_ref[0]) + sh_ref[0]).astype(BF16)


def _embed(x, g, b, sc, sh):
    bsz, s, d = x.shape
    ts = 256
    row = pl.BlockSpec((1, ts, d), lambda i, j: (i, j, 0))
    vec = pl.BlockSpec((1, d), lambda i, j: (0, 0))
    bvec = pl.BlockSpec((1, 1, d), lambda i, j: (i, 0, 0))
    blocks = 2 * _nbytes((ts, d), F32) + _nbytes((ts, d), BF16)
    return pl.pallas_call(
        _embed_kernel,
        grid=(bsz, s // ts),
        in_specs=[row, vec, vec, bvec, bvec],
        out_specs=[row, row],
        out_shape=[jax.ShapeDtypeStruct((bsz, s, d), F32), jax.ShapeDtypeStruct((bsz, s, d), BF16)],
        compiler_params=_params(("parallel", "parallel"), _vmem_limit(blocks, temps=4 * _nbytes((ts, d), F32))),
        name="embed_ln",
    )(x, g.reshape(1, d), b.reshape(1, d), sc, sh)


def _qkv_kernel(h_ref, w_ref, o_ref):
    o_ref[...] = _dot(h_ref[...], w_ref[...]).astype(o_ref.dtype)


def _qkv_proj(h2d, w_in, n_cols):
    m, d = h2d.shape
    tm, tn = _tile(m, 1024), _tile(n_cols, 1024)
    blocks = _nbytes((tm, d), BF16) + _nbytes((d, tn), BF16) + _nbytes((tm, tn), BF16)
    return pl.pallas_call(
        _qkv_kernel,
        grid=(n_cols // tn, m // tm),
        in_specs=[
            pl.BlockSpec((tm, d), lambda n, i: (i, 0)),
            pl.BlockSpec((d, tn), lambda n, i: (0, n)),
        ],
        out_specs=pl.BlockSpec((tm, tn), lambda n, i: (i, n)),
        out_shape=jax.ShapeDtypeStruct((m, n_cols), BF16),
        compiler_params=_params(("parallel", "parallel"), _vmem_limit(blocks, temps=2 * _nbytes((tm, tn), F32))),
        name="qkv_proj",
    )(h2d, w_in)


def _glu_kernel(h_ref, wv_ref, wg_ref, o_ref):
    h = h_ref[...]
    o_ref[...] = _dot(h, wv_ref[...]) * jax.nn.sigmoid(_dot(h, wg_ref[...]))


def _glu_proj(h2d, w_in, val_col, gate_col, width):
    m, d = h2d.shape
    tm, tn = _tile(m, 1024), _tile(width, 512)
    v_blk, g_blk = val_col // tn, gate_col // tn
    blocks = _nbytes((tm, d), BF16) + 2 * _nbytes((d, tn), BF16) + _nbytes((tm, tn), F32)
    return pl.pallas_call(
        _glu_kernel,
        grid=(width // tn, m // tm),
        in_specs=[
            pl.BlockSpec((tm, d), lambda n, i: (i, 0)),
            pl.BlockSpec((d, tn), lambda n, i: (0, v_blk + n)),
            pl.BlockSpec((d, tn), lambda n, i: (0, g_blk + n)),
        ],
        out_specs=pl.BlockSpec((tm, tn), lambda n, i: (i, n)),
        out_shape=jax.ShapeDtypeStruct((m, width), F32),
        compiler_params=_params(("parallel", "parallel"), _vmem_limit(blocks, temps=4 * _nbytes((tm, tn), F32))),
        name="glu_proj",
    )(h2d, w_in, w_in)


def _bias_tiles(n_heads, dilation):
    slopes = 2.0 ** (-8.0 * jnp.arange(1, n_heads + 1, dtype=F32) / n_heads)
    q = jnp.arange(Q_BLOCK)[:, None]
    c = jnp.arange(K_WINDOW)[None, :]
    tiles = []
    for key_start_minus_q_start in (0, -HALF_WINDOW, -2 * HALF_WINDOW):
        rel = key_start_minus_q_start + c - q
        dist = (dilation * jnp.abs(rel)).astype(F32)
        bias = -slopes[:, None, None] * dist[None]
        tiles.append(jnp.where((jnp.abs(rel) <= HALF_WINDOW)[None], bias, NEG_INF))
    b = jnp.stack(tiles, axis=0)
    b = b.reshape(3, n_heads // 2, 2, Q_BLOCK, K_WINDOW)
    return b.transpose(1, 0, 2, 3, 4)


def _attn_kernel(q_ref, k_ref, v_ref, bias_ref, o_ref, lse_ref, *, seq):
    lane = lax.broadcasted_iota(jnp.int32, (1, LANES), 1)
    first_head = lane < HEAD_DIM

    def block(i, carry):
        q0 = pl.multiple_of(i * Q_BLOCK, Q_BLOCK)
        k0 = pl.multiple_of(jnp.clip(q0 - HALF_WINDOW, 0, seq - K_WINDOW), HALF_WINDOW)
        kind = (q0 - k0) // HALF_WINDOW
        q = q_ref[0, 0, pl.ds(q0, Q_BLOCK), :] * jnp.asarray(HEAD_DIM ** -0.5, BF16)
        kw = k_ref[0, 0, pl.ds(k0, K_WINDOW), :]
        vw = v_ref[0, 0, pl.ds(k0, K_WINDOW), :]
        outs, lses = [], []
        for head in range(2):
            mine = first_head if head == 0 else jnp.logical_not(first_head)
            qh = jnp.where(mine, q, jnp.zeros_like(q))
            s = lax.dot_general(qh, kw, (((1,), (1,)), ((), ())), preferred_element_type=F32)
            s = s + bias_ref[0, kind, head]
            m = jnp.max(s, axis=-1, keepdims=True)
            p = jnp.exp(s - m)
            l = jnp.sum(p, axis=-1, keepdims=True)
            o = _dot(p.astype(BF16), vw)
            outs.append(o / l)
            lses.append(m + jnp.log(l))
        o_ref[0, 0, pl.ds(q0, Q_BLOCK), :] = jnp.where(first_head, outs[0], outs[1]).astype(o_ref.dtype)
        lse_ref[0, 0, pl.ds(q0, Q_BLOCK), :] = jnp.where(first_head, lses[0], lses[1])
        return carry

    lax.fori_loop(0, seq // Q_BLOCK, block, 0)


def _attn_branch(qkv, bias, attn_width):
    bsz, dil, seq, _ = qkv.shape
    n_pairs = attn_width // LANES
    assert seq % Q_BLOCK == 0 and seq >= K_WINDOW

    def slab(col0):
        return pl.BlockSpec((1, 1, seq, LANES), lambda p, b, r: (b, r, 0, col0 + p))

    out_spec = pl.BlockSpec((1, 1, seq, LANES), lambda p, b, r: (b, r, 0, p))
    bias_spec = pl.BlockSpec((1, 3, 2, Q_BLOCK, K_WINDOW), lambda p, b, r: (p, 0, 0, 0, 0))
    blocks = (3 * _nbytes((seq, LANES), BF16) + _nbytes((3, 2, Q_BLOCK, K_WINDOW), F32)
              + _nbytes((seq, LANES), BF16) + _nbytes((seq, LANES), F32))
    return pl.pallas_call(
        functools.partial(_attn_kernel, seq=seq),
        grid=(n_pairs, bsz, dil),
        in_specs=[slab(0), slab(n_pairs), slab(2 * n_pairs), bias_spec],
        out_specs=[out_spec, out_spec],
        out_shape=[jax.ShapeDtypeStruct((bsz, dil, seq, attn_width), BF16),
                   jax.ShapeDtypeStruct((bsz, dil, seq, attn_width), F32)],
        compiler_params=_params(("parallel", "parallel", "parallel"),
                                _vmem_limit(blocks, temps=16 * _nbytes((Q_BLOCK, K_WINDOW), F32))),
        name=f"attn_dil{dil}",
    )(qkv, qkv, qkv, bias)


CONV_HALO = 16
CONV_ROWS = 32
CONV_LANES = 256


def _conv_kernel(prev_ref, cur_ref, next_ref, w_ref, cb_ref, lg_ref, lb_ref, og_ref, o_ref, ext_ref, y_ref,
                 *, rows):
    j = pl.program_id(1)
    width = cur_ref.shape[-1]
    zeros = jnp.zeros((CONV_HALO, width), F32)
    ext_ref[0:CONV_HALO, :] = jnp.where(j > 0, prev_ref[0], zeros)
    ext_ref[CONV_HALO:CONV_HALO + rows, :] = cur_ref[0]
    ext_ref[CONV_HALO + rows:, :] = jnp.where(j < pl.num_programs(1) - 1, next_ref[0], zeros)

    for r0 in range(0, rows, CONV_ROWS):
        for c0 in range(0, width, CONV_LANES):
            acc = jnp.zeros((CONV_ROWS, CONV_LANES), F32)
            for tap in range(CONV_KERNEL):
                start = r0 + tap + (CONV_HALO - CONV_PAD)
                window = ext_ref[start:start + CONV_ROWS, c0:c0 + CONV_LANES]
                acc = acc + window * w_ref[tap:tap + 1, c0:c0 + CONV_LANES]
            y_ref[:, c0:c0 + CONV_LANES] = acc
        y = y_ref[...] + cb_ref[...]
        y = _layer_norm(y, lg_ref[...], lb_ref[...])
        y = y * jax.nn.sigmoid(y)
        o_ref[0, r0:r0 + CONV_ROWS, :] = _rms_norm(y, og_ref[...]).astype(o_ref.dtype)


def _conv_branch(u, conv_w, conv_b, ln_g, ln_b, out_g):
    bsz, s, width = u.shape
    rows = 256
    halo_blocks = rows // CONV_HALO
    last_halo = s // CONV_HALO - 1
    vec = pl.BlockSpec((1, width), lambda b, j: (0, 0))
    blocks = (2 * _nbytes((CONV_HALO, width), F32) + _nbytes((rows, width), F32)
              + _nbytes((CONV_KERNEL, width), F32) + _nbytes((rows, width), BF16))
    scratch = _nbytes((rows + 2 * CONV_HALO, width), F32) + _nbytes((CONV_ROWS, width), F32)
    return pl.pallas_call(
        functools.partial(_conv_kernel, rows=rows),
        grid=(bsz, s // rows),
        in_specs=[
            pl.BlockSpec((1, CONV_HALO, width), lambda b, j: (b, jnp.maximum(j * halo_blocks - 1, 0), 0)),
            pl.BlockSpec((1, rows, width), lambda b, j: (b, j, 0)),
            pl.BlockSpec((1, CONV_HALO, width), lambda b, j: (b, jnp.minimum((j + 1) * halo_blocks, last_halo), 0)),
            pl.BlockSpec((CONV_KERNEL, width), lambda b, j: (0, 0)),
            vec, vec, vec, vec,
        ],
        out_specs=pl.BlockSpec((1, rows, width), lambda b, j: (b, j, 0)),
        out_shape=jax.ShapeDtypeStruct((bsz, s, width), BF16),
        scratch_shapes=[pltpu.VMEM((rows + 2 * CONV_HALO, width), F32), pltpu.VMEM((CONV_ROWS, width), F32)],
        compiler_params=_params(("parallel", "parallel"),
                                _vmem_limit(blocks, scratch=scratch, temps=6 * _nbytes((rows, width), F32))),
        name="conv_branch",
    )(u, u, u, conv_w, conv_b.reshape(1, width), ln_g.reshape(1, width), ln_b.reshape(1, width),
      out_g.reshape(1, width))


def _out_kernel(o1_ref, o2_ref, o3_ref, l1_ref, l2_ref, l3_ref, uc_ref, x_ref, w_ref, ag_ref, gate_ref,
                lg_ref, lb_ref, sc_ref, sh_ref, x_out, h_out, *, alpha):
    l1, l2, l3 = l1_ref[0], l2_ref[0], l3_ref[0]
    top = jnp.maximum(jnp.maximum(l1, l2), l3)
    e1, e2, e3 = jnp.exp(l1 - top), jnp.exp(l2 - top), jnp.exp(l3 - top)
    attn = (e1 * o1_ref[0].astype(F32) + e2 * o2_ref[0].astype(F32) + e3 * o3_ref[0].astype(F32)) / (e1 + e2 + e3)
    merged = jnp.concatenate([_rms_norm(attn, ag_ref[...]).astype(BF16), uc_ref[0]], axis=-1)
    mix = _dot(merged, w_ref[...])
    x = _layer_norm(alpha * x_ref[0] + (1.0 + gate_ref[0]) * mix, lg_ref[...], lb_ref[...])
    x_out[0] = x
    h_out[0] = (x * (1.0 + sc_ref[0]) + sh_ref[0]).astype(BF16)


def _out_proj(o_branches, lse_branches, uc, x, w_out, attn_g, gate, ln_g, ln_b, sc, sh, alpha):
    bsz, s, d = x.shape
    aw = o_branches[0].shape[-1]
    cw = uc.shape[-1]
    tm = 256
    def rows(width):
        return pl.BlockSpec((1, tm, width), lambda b, i: (b, i, 0))
    def vec(width):
        return pl.BlockSpec((1, width), lambda b, i: (0, 0))
    bvec = pl.BlockSpec((1, 1, d), lambda b, i: (b, 0, 0))
    blocks = (3 * _nbytes((tm, aw), BF16) + 3 * _nbytes((tm, aw), F32) + _nbytes((tm, cw), BF16)
              + 2 * _nbytes((tm, d), F32) + _nbytes((tm, d), BF16) + _nbytes((d, d), BF16))
    return pl.pallas_call(
        functools.partial(_out_kernel, alpha=alpha),
        grid=(bsz, s // tm),
        in_specs=[rows(aw), rows(aw), rows(aw), rows(aw), rows(aw), rows(aw), rows(cw), rows(d),
                  pl.BlockSpec((d, d), lambda b, i: (0, 0)), vec(aw), bvec, vec(d), vec(d), bvec, bvec],
        out_specs=[rows(d), rows(d)],
        out_shape=[jax.ShapeDtypeStruct((bsz, s, d), F32), jax.ShapeDtypeStruct((bsz, s, d), BF16)],
        compiler_params=_params(("parallel", "parallel"), _vmem_limit(blocks, temps=6 * _nbytes((tm, d), F32))),
        name="out_proj",
    )(*o_branches, *lse_branches, uc, x, w_out, attn_g.reshape(1, aw), gate, ln_g.reshape(1, d),
      ln_b.reshape(1, d), sc, sh)


def _mlp_kernel(h_ref, w1_ref, b1_ref, w2_ref, b2_ref, x_ref, gate_ref, lg_ref, lb_ref, *rest, alpha, modulate):
    if modulate:
        sc_ref, sh_ref, x_out, h_out, acc_ref = rest
    else:
        x_out, acc_ref = rest
    f = pl.program_id(2)

    @pl.when(f == 0)
    def _():
        acc_ref[...] = jnp.zeros_like(acc_ref)

    hid = jnp.maximum(_dot(h_ref[0], w1_ref[...]) + b1_ref[...], 0.0)
    acc_ref[...] += _dot((hid * hid).astype(BF16), w2_ref[...])

    @pl.when(f == pl.num_programs(2) - 1)
    def _():
        ff = acc_ref[...] + b2_ref[...]
        x = _layer_norm(alpha * x_ref[0] + (1.0 + gate_ref[0]) * ff, lg_ref[...], lb_ref[...])
        x_out[0] = x
        if modulate:
            h_out[0] = (x * (1.0 + sc_ref[0]) + sh_ref[0]).astype(BF16)


def _mlp(h, w1, b1, w2, b2, x, gate, ln_g, ln_b, alpha, next_mod=None):
    bsz, s, d = x.shape
    ff_dim = w1.shape[1]
    tm, tf = _tile(s, 512), _tile(ff_dim, 1024)
    modulate = next_mod is not None
    row_f32 = pl.BlockSpec((1, tm, d), lambda b, i, f: (b, i, 0))
    vec_d = pl.BlockSpec((1, d), lambda b, i, f: (0, 0))
    bvec = pl.BlockSpec((1, 1, d), lambda b, i, f: (b, 0, 0))
    in_specs = [
        row_f32,
        pl.BlockSpec((d, tf), lambda b, i, f: (0, f)),
        pl.BlockSpec((1, tf), lambda b, i, f: (0, f)),
        pl.BlockSpec((tf, d), lambda b, i, f: (f, 0)),
        vec_d, row_f32, bvec, vec_d, vec_d,
    ]
    args = [h, w1, b1.reshape(1, ff_dim), w2, b2.reshape(1, d), x, gate, ln_g.reshape(1, d), ln_b.reshape(1, d)]
    out_specs = [row_f32]
    out_shape = [jax.ShapeDtypeStruct((bsz, s, d), F32)]
    blocks = (_nbytes((tm, d), BF16) + 2 * _nbytes((d, tf), BF16) + 2 * _nbytes((tm, d), F32))
    if modulate:
        in_specs += [bvec, bvec]
        args += list(next_mod)
        out_specs.append(row_f32)
        out_shape.append(jax.ShapeDtypeStruct((bsz, s, d), BF16))
        blocks += _nbytes((tm, d), BF16)
    out = pl.pallas_call(
        functools.partial(_mlp_kernel, alpha=alpha, modulate=modulate),
        grid=(bsz, s // tm, ff_dim // tf),
        in_specs=in_specs,
        out_specs=out_specs,
        out_shape=out_shape,
        scratch_shapes=[pltpu.VMEM((tm, d), F32)],
        compiler_params=_params(("parallel", "parallel", "arbitrary"),
                                _vmem_limit(blocks, scratch=_nbytes((tm, d), F32), temps=3 * _nbytes((tm, tf), F32))),
        name="mlp",
    )(*args)
    return out if modulate else (out[0], None)


def _to_strided(t, dilation):
    bsz, s, c = t.shape
    if dilation == 1:
        return t.reshape(bsz, 1, s, c)
    return t.reshape(bsz, s // dilation, dilation, c).transpose(0, 2, 1, 3)


def _from_strided(t):
    bsz, dilation, length, c = t.shape
    if dilation == 1:
        return t.reshape(bsz, length, c)
    return t.transpose(0, 2, 1, 3).reshape(bsz, dilation * length, c)


def _trunk(x, mods, emb_ln_g, emb_ln_b, w_in, conv_w, conv_b, conv_ln_g, conv_ln_b, attn_out_g, conv_out_g,
           w_out, ln1_g, ln1_b, w_mlp1, b_mlp1, w_mlp2, b_mlp2, ln2_g, ln2_b, biases):
    bsz, s, d = x.shape
    depth = w_in.shape[0]
    aw = d // 2
    cw = d - aw
    alpha = (2 * depth) ** 0.25

    def mod(layer, k):
        return mods[layer, :, k * d:(k + 1) * d].reshape(bsz, 1, d)

    x, h = _embed(x, emb_ln_g, emb_ln_b, mod(0, 1), mod(0, 0))
    for layer in range(depth):
        h2d = h.reshape(bsz * s, d)
        qkv = _qkv_proj(h2d, w_in[layer], 3 * aw).reshape(bsz, s, 3 * aw)
        u = _glu_proj(h2d, w_in[layer], 3 * aw, 3 * aw + cw, cw).reshape(bsz, s, cw)
        outs, lses = [], []
        for (window, dilation), bias in zip(DILATED_BRANCHES, biases):
            assert window == 2 * dilation * HALF_WINDOW
            o, lse = _attn_branch(_to_strided(qkv, dilation), bias, aw)
            outs.append(_from_strided(o))
            lses.append(_from_strided(lse))
        uc = _conv_branch(u, conv_w[layer], conv_b[layer], conv_ln_g[layer], conv_ln_b[layer], conv_out_g[layer])
        x, h = _out_proj(outs, lses, uc, x, w_out[layer], attn_out_g[layer], mod(layer, 2), ln1_g[layer],
                         ln1_b[layer], mod(layer, 4), mod(layer, 3), alpha)
        next_mod = (mod(layer + 1, 1), mod(layer + 1, 0)) if layer + 1 < depth else None
        x, h = _mlp(h, w_mlp1[layer], b_mlp1[layer], w_mlp2[layer], b_mlp2[layer], x, mod(layer, 5),
                    ln2_g[layer], ln2_b[layer], alpha, next_mod)
    return x


def kernel(x_prompt, x_sample, c_prompt, c_sample, emb_ln_g, emb_ln_b, w_ada, b_ada, w_in, conv_w, conv_b,
           conv_ln_g, conv_ln_b, attn_out_g, conv_out_g, w_out, ln1_g, ln1_b, w_mlp1, b_mlp1, w_mlp2, b_mlp2,
           ln2_g, ln2_b):
    d = x_prompt.shape[-1]
    n_prompt, n_sample = c_prompt.shape[0], c_sample.shape[0]
    pad = (-(n_prompt + n_sample)) % 8
    c_all = jnp.concatenate([c_prompt, c_sample, jnp.zeros((pad, d), F32)], axis=0)
    mods = _ada_mod(c_all, w_ada, b_ada)
    n_heads = (d // 2) // HEAD_DIM
    biases = [_bias_tiles(n_heads, dilation) for _, dilation in DILATED_BRANCHES]
    weights = (emb_ln_g, emb_ln_b, w_in.astype(BF16), conv_w, conv_b, conv_ln_g, conv_ln_b, attn_out_g,
               conv_out_g, w_out.astype(BF16), ln1_g, ln1_b, w_mlp1.astype(BF16), b_mlp1, w_mlp2.astype(BF16),
               b_mlp2, ln2_g, ln2_b, biases)
    y_prompt = _trunk(x_prompt, mods[:, :n_prompt], *weights)
    y_sample = _trunk(x_sample, mods[:, n_prompt:n_prompt + n_sample], *weights)
    return (y_prompt, y_sample)
```

```python
import functools

import jax
import jax.numpy as jnp
from jax import lax
from jax.experimental import pallas as pl
from jax.experimental.pallas import tpu as pltpu

F32 = jnp.float32
BF16 = jnp.bfloat16

HEAD_DIM = 64
DILATED_BRANCHES = ((128, 1), (512, 4), (2048, 16))
HALF_WINDOW = 64
CONV_KERNEL = 31
CONV_PAD = (CONV_KERNEL - 1) // 2
N_MOD = 6
LN_EPS = 1e-5
NEG_INF = -1e30

LANES = 128
SUBLANES = 8
VMEM_BYTES_V7X = 64 * 1024 * 1024
VMEM_CAP = VMEM_BYTES_V7X - 8 * 1024 * 1024

Q_BLOCK = 128
K_WINDOW = Q_BLOCK + 2 * HALF_WINDOW
ATTN_UNROLL = 4
ATTN_GATHER_ROWS = 256


def _nbytes(shape, dtype):
    n = 1
    for s in shape:
        n *= s
    return n * jnp.dtype(dtype).itemsize


def _vmem_limit(pipelined, scratch=0, temps=0):
    return int(min(VMEM_CAP, 2 * pipelined + scratch + temps + (2 << 20)))


def _tile(n, preferred, align=LANES):
    t = min(preferred, n) // align * align
    while n % t:
        t -= align
    return t


def _params(semantics, vmem):
    return pltpu.CompilerParams(dimension_semantics=semantics, vmem_limit_bytes=vmem)


def _layer_norm(y, g, b):
    mu = jnp.mean(y, axis=-1, keepdims=True)
    yc = y - mu
    var = jnp.mean(yc * yc, axis=-1, keepdims=True)
    return yc * lax.rsqrt(var + LN_EPS) * g + b


def _rms_norm(y, g):
    return y * lax.rsqrt(jnp.mean(y * y, axis=-1, keepdims=True) + LN_EPS) * g


def _dot(a, b):
    return jnp.dot(a, b, preferred_element_type=F32)


def _ada_kernel(c_ref, w_ref, b_ref, o_ref):
    c = c_ref[...]
    a = c * jax.nn.sigmoid(c)
    w = w_ref[0]
    a_hi = a.astype(BF16)
    a_lo = (a - a_hi.astype(F32)).astype(BF16)
    w_hi = w.astype(BF16)
    w_lo = (w - w_hi.astype(F32)).astype(BF16)
    o_ref[0] = _dot(a_hi, w_hi) + _dot(a_hi, w_lo) + _dot(a_lo, w_hi) + b_ref[0]


def _ada_mod(c, w_ada, b_ada):
    depth, d, n = w_ada.shape
    rows = c.shape[0]
    tn = _tile(n, 1024)
    blocks = _nbytes((rows, d), F32) + _nbytes((d, tn), F32) + 2 * _nbytes((8, tn), F32)
    return pl.pallas_call(
        _ada_kernel,
        grid=(depth, n // tn),
        in_specs=[
            pl.BlockSpec((rows, d), lambda l, j: (0, 0)),
            pl.BlockSpec((1, d, tn), lambda l, j: (l, 0, j)),
            pl.BlockSpec((1, 1, tn), lambda l, j: (l, 0, j)),
        ],
        out_specs=pl.BlockSpec((1, rows, tn), lambda l, j: (l, 0, j)),
        out_shape=jax.ShapeDtypeStruct((depth, rows, n), F32),
        compiler_params=_params(("parallel", "parallel"), _vmem_limit(blocks, temps=2 * _nbytes((d, tn), F32))),
        name="ada_mod",
    )(c, w_ada, b_ada.reshape(depth, 1, n))


def _embed_kernel(x_ref, g_ref, b_ref, sc_ref, sh_ref, x_out, h_out):
    x = _layer_norm(x_ref[0], g_ref[...], b_ref[...])
    x_out[0] = x
    h_out[0] = (x * (1.0 + sc_ref[0]) + sh_ref[0]).astype(BF16)


def _embed(x, g, b, sc, sh):
    bsz, s, d = x.shape
    ts = 256
    row = pl.BlockSpec((1, ts, d), lambda i, j: (i, j, 0))
    vec = pl.BlockSpec((1, d), lambda i, j: (0, 0))
    bvec = pl.BlockSpec((1, 1, d), lambda i, j: (i, 0, 0))
    blocks = 2 * _nbytes((ts, d), F32) + _nbytes((ts, d), BF16)
    return pl.pallas_call(
        _embed_kernel,
        grid=(bsz, s // ts),
        in_specs=[row, vec, vec, bvec, bvec],
        out_specs=[row, row],
        out_shape=[jax.ShapeDtypeStruct((bsz, s, d), F32), jax.ShapeDtypeStruct((bsz, s, d), BF16)],
        compiler_params=_params(("parallel", "parallel"), _vmem_limit(blocks, temps=4 * _nbytes((ts, d), F32))),
        name="embed_ln",
    )(x, g.reshape(1, d), b.reshape(1, d), sc, sh)


def _qkv_kernel(h_ref, w_ref, o_ref):
    o_ref[...] = _dot(h_ref[...], w_ref[...]).astype(o_ref.dtype)


def _qkv_proj(h2d, w_in, n_cols):
    m, d = h2d.shape
    tm, tn = _tile(m, 1024), _tile(n_cols, 1024)
    blocks = _nbytes((tm, d), BF16) + _nbytes((d, tn), BF16) + _nbytes((tm, tn), F32)
    return pl.pallas_call(
        _qkv_kernel,
        grid=(n_cols // tn, m // tm),
        in_specs=[
            pl.BlockSpec((tm, d), lambda n, i: (i, 0)),
            pl.BlockSpec((d, tn), lambda n, i: (0, n)),
        ],
        out_specs=pl.BlockSpec((tm, tn), lambda n, i: (i, n)),
        out_shape=jax.ShapeDtypeStruct((m, n_cols), F32),
        compiler_params=_params(("parallel", "parallel"), _vmem_limit(blocks, temps=2 * _nbytes((tm, tn), F32))),
        name="qkv_proj",
    )(h2d, w_in)


def _glu_kernel(h_ref, wv_ref, wg_ref, o_ref):
    h = h_ref[...]
    o_ref[...] = _dot(h, wv_ref[...]) * jax.nn.sigmoid(_dot(h, wg_ref[...]))


def _glu_proj(h2d, w_in, val_col, gate_col, width):
    m, d = h2d.shape
    tm, tn = _tile(m, 1024), _tile(width, 512)
    v_blk, g_blk = val_col // tn, gate_col // tn
    blocks = _nbytes((tm, d), BF16) + 2 * _nbytes((d, tn), BF16) + _nbytes((tm, tn), F32)
    return pl.pallas_call(
        _glu_kernel,
        grid=(width // tn, m // tm),
        in_specs=[
            pl.BlockSpec((tm, d), lambda n, i: (i, 0)),
            pl.BlockSpec((d, tn), lambda n, i: (0, v_blk + n)),
            pl.BlockSpec((d, tn), lambda n, i: (0, g_blk + n)),
        ],
        out_specs=pl.BlockSpec((tm, tn), lambda n, i: (i, n)),
        out_shape=jax.ShapeDtypeStruct((m, width), F32),
        compiler_params=_params(("parallel", "parallel"), _vmem_limit(blocks, temps=4 * _nbytes((tm, tn), F32))),
        name="glu_proj",
    )(h2d, w_in, w_in)


def _bias_tiles(n_heads):
    slopes = 2.0 ** (-8.0 * jnp.arange(1, n_heads + 1, dtype=F32) / n_heads)
    q = jnp.arange(Q_BLOCK)[:, None]
    c = jnp.arange(K_WINDOW)[None, :]
    branches = []
    for window, dilation in DILATED_BRANCHES:
        assert window == 2 * dilation * HALF_WINDOW
        tiles = []
        for key_start_minus_q_start in (0, -HALF_WINDOW, -2 * HALF_WINDOW):
            rel = key_start_minus_q_start + c - q
            dist = (dilation * jnp.abs(rel)).astype(F32)
            bias = -slopes[:, None, None] * dist[None]
            tiles.append(jnp.where((jnp.abs(rel) <= HALF_WINDOW)[None], bias, NEG_INF))
        branches.append(jnp.stack(tiles, axis=0))
    b = jnp.stack(branches, axis=0)
    b = b.reshape(len(DILATED_BRANCHES), 3, n_heads // 2, 2, Q_BLOCK, K_WINDOW)
    return b.transpose(2, 0, 1, 3, 4, 5)


def _attn_kernel(q_ref, k_ref, v_ref, bias_ref, o_ref, qs_ref, ks_ref, vs_ref, acc_ref, m_ref, l_ref, *, seq):
    lane = lax.broadcasted_iota(jnp.int32, (1, LANES), 1)
    first_head = lane < HEAD_DIM
    scale = HEAD_DIM ** -0.5
    n_branches = len(DILATED_BRANCHES)

    def rows_of(cls, dil, start, size):
        if dil == 1:
            return pl.ds(start, size)
        return pl.ds(cls + dil * start, size, stride=dil)

    for branch, (_, dil) in enumerate(DILATED_BRANCHES):
        length = seq // dil
        unroll = _tile(length // Q_BLOCK, ATTN_UNROLL, align=1)
        gather_rows = min(ATTN_GATHER_ROWS, length)

        def gather(cls, dil=dil, length=length, gather_rows=gather_rows):
            def step(c, carry):
                c0 = pl.multiple_of(c * gather_rows, gather_rows)
                rows = rows_of(cls, dil, c0, gather_rows)
                qs_ref[pl.ds(c0, gather_rows), :] = (q_ref[0, rows, :] * scale).astype(BF16)
                ks_ref[pl.ds(c0, gather_rows), :] = k_ref[0, rows, :].astype(BF16)
                vs_ref[pl.ds(c0, gather_rows), :] = v_ref[0, rows, :].astype(BF16)
                return carry
            lax.fori_loop(0, length // gather_rows, step, 0)

        def block(cls, i, branch=branch, dil=dil, length=length):
            q0 = pl.multiple_of(i * Q_BLOCK, Q_BLOCK)
            k0 = pl.multiple_of(jnp.clip(q0 - HALF_WINDOW, 0, length - K_WINDOW), HALF_WINDOW)
            kind = (q0 - k0) // HALF_WINDOW
            if dil == 1:
                q = (q_ref[0, pl.ds(q0, Q_BLOCK), :] * scale).astype(BF16)
                kw = k_ref[0, pl.ds(k0, K_WINDOW), :].astype(BF16)
                vw = v_ref[0, pl.ds(k0, K_WINDOW), :].astype(BF16)
            else:
                q = qs_ref[pl.ds(q0, Q_BLOCK), :]
                kw = ks_ref[pl.ds(k0, K_WINDOW), :]
                vw = vs_ref[pl.ds(k0, K_WINDOW), :]
            tops, dens, nums = [], [], []
            for head in range(2):
                mine = first_head if head == 0 else jnp.logical_not(first_head)
                qh = jnp.where(mine, q, jnp.zeros_like(q))
                s = lax.dot_general(qh, kw, (((1,), (1,)), ((), ())), preferred_element_type=F32)
                s = s + bias_ref[0, branch, kind, head]
                m = jnp.max(s, axis=-1, keepdims=True)
                p = jnp.exp(s - m)
                tops.append(m)
                dens.append(jnp.sum(p, axis=-1, keepdims=True))
                nums.append(_dot(p.astype(BF16), vw))
            m_blk = jnp.where(first_head, tops[0], tops[1])
            l_blk = jnp.where(first_head, dens[0], dens[1])
            o_blk = jnp.where(first_head, nums[0], nums[1])
            rows = rows_of(cls, dil, q0, Q_BLOCK)
            if branch == 0:
                m_new, l_new, o_new = m_blk, l_blk, o_blk
            else:
                m_old = m_ref[rows, :]
                m_new = jnp.maximum(m_old, m_blk)
                w_old = jnp.exp(m_old - m_new)
                w_blk = jnp.exp(m_blk - m_new)
                l_new = w_old * l_ref[rows, :] + w_blk * l_blk
                o_new = w_old * acc_ref[rows, :] + w_blk * o_blk
            if branch == n_branches - 1:
                acc_ref[rows, :] = o_new / l_new
            else:
                m_ref[rows, :] = m_new
                l_ref[rows, :] = l_new
                acc_ref[rows, :] = o_new

        def one_class(cls, carry, dil=dil, length=length, unroll=unroll, gather=gather, block=block):
            if dil > 1:
                gather(cls)

            def group(g, carry2):
                for u in range(unroll):
                    block(cls, g * unroll + u)
                return carry2

            lax.fori_loop(0, length // (Q_BLOCK * unroll), group, 0)
            return carry

        if dil == 1:
            one_class(0, 0)
        else:
            lax.fori_loop(0, dil, one_class, 0)

    def emit(c, carry):
        c0 = pl.multiple_of(c * ATTN_GATHER_ROWS, ATTN_GATHER_ROWS)
        o_ref[0, pl.ds(c0, ATTN_GATHER_ROWS), :] = acc_ref[pl.ds(c0, ATTN_GATHER_ROWS), :].astype(o_ref.dtype)
        return carry

    lax.fori_loop(0, seq // ATTN_GATHER_ROWS, emit, 0)


def _attention(qkv, bias, attn_width):
    bsz, seq, _ = qkv.shape
    n_pairs = attn_width // LANES
    max_dil = max(d for _, d in DILATED_BRANCHES)
    min_strided_dil = min(d for _, d in DILATED_BRANCHES if d > 1)
    assert seq % (max_dil * Q_BLOCK) == 0 and seq // max_dil >= K_WINDOW and seq % ATTN_GATHER_ROWS == 0

    def slab(col0):
        return pl.BlockSpec((1, seq, LANES), lambda p, b: (b, 0, col0 + p))

    bias_block = (1,) + bias.shape[1:]
    strided = (seq // min_strided_dil, LANES)
    blocks = 3 * _nbytes((seq, LANES), F32) + _nbytes(bias_block, F32) + _nbytes((seq, LANES), BF16)
    scratch = 3 * _nbytes(strided, BF16) + 3 * _nbytes((seq, LANES), F32)
    return pl.pallas_call(
        functools.partial(_attn_kernel, seq=seq),
        grid=(n_pairs, bsz),
        in_specs=[slab(0), slab(n_pairs), slab(2 * n_pairs),
                  pl.BlockSpec(bias_block, lambda p, b: (p, 0, 0, 0, 0, 0))],
        out_specs=pl.BlockSpec((1, seq, LANES), lambda p, b: (b, 0, p)),
        out_shape=jax.ShapeDtypeStruct((bsz, seq, attn_width), BF16),
        scratch_shapes=[pltpu.VMEM(strided, BF16)] * 3 + [pltpu.VMEM((seq, LANES), F32)] * 3,
        compiler_params=_params(("parallel", "parallel"),
                                _vmem_limit(blocks, scratch=scratch, temps=32 * _nbytes((Q_BLOCK, K_WINDOW), F32))),
        name="attention",
    )(qkv, qkv, qkv, bias)


CONV_HALO = 16
CONV_ROWS = 32
CONV_LANES = 256


def _conv_kernel(prev_ref, cur_ref, next_ref, w_ref, cb_ref, lg_ref, lb_ref, og_ref, o_ref, sh_ref, y_ref,
                 *, rows):
    j = pl.program_id(1)
    width = cur_ref.shape[-1]
    zeros = jnp.zeros((CONV_HALO, width), F32)
    sh_ref[0, 0:CONV_HALO, :] = jnp.where(j > 0, prev_ref[0], zeros)
    sh_ref[0, CONV_HALO:CONV_HALO + rows, :] = cur_ref[0]
    sh_ref[0, CONV_HALO + rows:, :] = jnp.where(j < pl.num_programs(1) - 1, next_ref[0], zeros)
    shifted_rows = rows + 2 * CONV_HALO - SUBLANES
    for b in range(1, SUBLANES):
        sh_ref[b, 0:shifted_rows, :] = sh_ref[0, b:b + shifted_rows, :]

    for r0 in range(0, rows, CONV_ROWS):
        for c0 in range(0, width, CONV_LANES):
            acc = jnp.zeros((CONV_ROWS // SUBLANES, SUBLANES, CONV_LANES), F32)
            for tap in range(CONV_KERNEL):
                offset = tap + (CONV_HALO - CONV_PAD)
                shift = offset % SUBLANES
                start = r0 + offset - shift
                window = sh_ref[shift, start:start + CONV_ROWS, c0:c0 + CONV_LANES]
                acc = acc + window.reshape(acc.shape) * w_ref[tap, :, c0:c0 + CONV_LANES][None]
            y_ref[:, c0:c0 + CONV_LANES] = acc.reshape(CONV_ROWS, CONV_LANES)
        y = y_ref[...] + cb_ref[...]
        y = _layer_norm(y, lg_ref[...], lb_ref[...])
        y = y * jax.nn.sigmoid(y)
        o_ref[0, r0:r0 + CONV_ROWS, :] = _rms_norm(y, og_ref[...]).astype(o_ref.dtype)


def _conv_branch(u, conv_w, conv_b, ln_g, ln_b, out_g):
    bsz, s, width = u.shape
    rows = 256
    halo_blocks = rows // CONV_HALO
    last_halo = s // CONV_HALO - 1
    vec = pl.BlockSpec((1, width), lambda b, j: (0, 0))
    blocks = (2 * _nbytes((CONV_HALO, width), F32) + _nbytes((rows, width), F32)
              + _nbytes((CONV_KERNEL, SUBLANES, width), F32) + _nbytes((rows, width), BF16))
    shifted = (SUBLANES, rows + 2 * CONV_HALO, width)
    scratch = _nbytes(shifted, F32) + _nbytes((CONV_ROWS, width), F32)
    w_tiles = jnp.broadcast_to(conv_w[:, None, :], (CONV_KERNEL, SUBLANES, width))
    return pl.pallas_call(
        functools.partial(_conv_kernel, rows=rows),
        grid=(bsz, s // rows),
        in_specs=[
            pl.BlockSpec((1, CONV_HALO, width), lambda b, j: (b, jnp.maximum(j * halo_blocks - 1, 0), 0)),
            pl.BlockSpec((1, rows, width), lambda b, j: (b, j, 0)),
            pl.BlockSpec((1, CONV_HALO, width), lambda b, j: (b, jnp.minimum((j + 1) * halo_blocks, last_halo), 0)),
            pl.BlockSpec((CONV_KERNEL, SUBLANES, width), lambda b, j: (0, 0, 0)),
            vec, vec, vec, vec,
        ],
        out_specs=pl.BlockSpec((1, rows, width), lambda b, j: (b, j, 0)),
        out_shape=jax.ShapeDtypeStruct((bsz, s, width), BF16),
        scratch_shapes=[pltpu.VMEM(shifted, F32), pltpu.VMEM((CONV_ROWS, width), F32)],
        compiler_params=_params(("parallel", "parallel"),
                                _vmem_limit(blocks, scratch=scratch, temps=6 * _nbytes((rows, width), F32))),
        name="conv_branch",
    )(u, u, u, w_tiles, conv_b.reshape(1, width), ln_g.reshape(1, width), ln_b.reshape(1, width),
      out_g.reshape(1, width))


def _out_kernel(at_ref, uc_ref, x_ref, w_ref, ag_ref, gate_ref, lg_ref, lb_ref, sc_ref, sh_ref, x_out, h_out,
                *, alpha):
    attn = _rms_norm(at_ref[0].astype(F32), ag_ref[...]).astype(BF16)
    mix = _dot(jnp.concatenate([attn, uc_ref[0]], axis=-1), w_ref[...])
    x = _layer_norm(alpha * x_ref[0] + (1.0 + gate_ref[0]) * mix, lg_ref[...], lb_ref[...])
    x_out[0] = x
    h_out[0] = (x * (1.0 + sc_ref[0]) + sh_ref[0]).astype(BF16)


def _out_proj(attn, uc, x, w_out, attn_g, gate, ln_g, ln_b, sc, sh, alpha):
    bsz, s, d = x.shape
    aw = attn.shape[-1]
    cw = uc.shape[-1]
    tm = _tile(s, 512)

    def rows(width):
        return pl.BlockSpec((1, tm, width), lambda b, i: (b, i, 0))

    def vec(width):
        return pl.BlockSpec((1, width), lambda b, i: (0, 0))

    bvec = pl.BlockSpec((1, 1, d), lambda b, i: (b, 0, 0))
    blocks = (_nbytes((tm, aw), BF16) + _nbytes((tm, cw), BF16) + 2 * _nbytes((tm, d), F32)
              + _nbytes((tm, d), BF16) + _nbytes((d, d), BF16))
    return pl.pallas_call(
        functools.partial(_out_kernel, alpha=alpha),
        grid=(bsz, s // tm),
        in_specs=[rows(aw), rows(cw), rows(d), pl.BlockSpec((d, d), lambda b, i: (0, 0)), vec(aw), bvec,
                  vec(d), vec(d), bvec, bvec],
        out_specs=[rows(d), rows(d)],
        out_shape=[jax.ShapeDtypeStruct((bsz, s, d), F32), jax.ShapeDtypeStruct((bsz, s, d), BF16)],
        compiler_params=_params(("parallel", "parallel"), _vmem_limit(blocks, temps=4 * _nbytes((tm, d), F32))),
        name="out_proj",
    )(attn, uc, x, w_out, attn_g.reshape(1, aw), gate, ln_g.reshape(1, d), ln_b.reshape(1, d), sc, sh)


def _mlp_kernel(h_ref, w1_ref, b1_ref, w2_ref, b2_ref, x_ref, gate_ref, lg_ref, lb_ref, *rest, alpha, modulate):
    if modulate:
        sc_ref, sh_ref, x_out, h_out, acc_ref = rest
    else:
        x_out, acc_ref = rest
    f = pl.program_id(2)

    @pl.when(f == 0)
    def _():
        acc_ref[...] = jnp.zeros_like(acc_ref)

    hid = jnp.maximum(_dot(h_ref[0], w1_ref[...]) + b1_ref[...], 0.0)
    acc_ref[...] += _dot((hid * hid).astype(BF16), w2_ref[...])

    @pl.when(f == pl.num_programs(2) - 1)
    def _():
        ff = acc_ref[...] + b2_ref[...]
        x = _layer_norm(alpha * x_ref[0] + (1.0 + gate_ref[0]) * ff, lg_ref[...], lb_ref[...])
        x_out[0] = x
        if modulate:
            h_out[0] = (x * (1.0 + sc_ref[0]) + sh_ref[0]).astype(BF16)


def _mlp(h, w1, b1, w2, b2, x, gate, ln_g, ln_b, alpha, next_mod=None):
    bsz, s, d = x.shape
    ff_dim = w1.shape[1]
    tm, tf = _tile(s, 512), _tile(ff_dim, 1024)
    modulate = next_mod is not None
    row_f32 = pl.BlockSpec((1, tm, d), lambda b, i, f: (b, i, 0))
    vec_d = pl.BlockSpec((1, d), lambda b, i, f: (0, 0))
    bvec = pl.BlockSpec((1, 1, d), lambda b, i, f: (b, 0, 0))
    in_specs = [
        row_f32,
        pl.BlockSpec((d, tf), lambda b, i, f: (0, f)),
        pl.BlockSpec((1, tf), lambda b, i, f: (0, f)),
        pl.BlockSpec((tf, d), lambda b, i, f: (f, 0)),
        vec_d, row_f32, bvec, vec_d, vec_d,
    ]
    args = [h, w1, b1.reshape(1, ff_dim), w2, b2.reshape(1, d), x, gate, ln_g.reshape(1, d), ln_b.reshape(1, d)]
    out_specs = [row_f32]
    out_shape = [jax.ShapeDtypeStruct((bsz, s, d), F32)]
    blocks = (_nbytes((tm, d), BF16) + 2 * _nbytes((d, tf), BF16) + 2 * _nbytes((tm, d), F32))
    if modulate:
        in_specs += [bvec, bvec]
        args += list(next_mod)
        out_specs.append(row_f32)
        out_shape.append(jax.ShapeDtypeStruct((bsz, s, d), BF16))
        blocks += _nbytes((tm, d), BF16)
    out = pl.pallas_call(
        functools.partial(_mlp_kernel, alpha=alpha, modulate=modulate),
        grid=(bsz, s // tm, ff_dim // tf),
        in_specs=in_specs,
        out_specs=out_specs,
        out_shape=out_shape,
        scratch_shapes=[pltpu.VMEM((tm, d), F32)],
        compiler_params=_params(("parallel", "parallel", "arbitrary"),
                                _vmem_limit(blocks, scratch=_nbytes((tm, d), F32), temps=3 * _nbytes((tm, tf), F32))),
        name="mlp",
    )(*args)
    return out if modulate else (out[0], None)


def _trunk(x, mods, emb_ln_g, emb_ln_b, w_in, conv_w, conv_b, conv_ln_g, conv_ln_b, attn_out_g, conv_out_g,
           w_out, ln1_g, ln1_b, w_mlp1, b_mlp1, w_mlp2, b_mlp2, ln2_g, ln2_b, bias):
    bsz, s, d = x.shape
    depth = w_in.shape[0]
    aw = d // 2
    cw = d - aw
    alpha = (2 * depth) ** 0.25

    def mod(layer, k):
        return mods[layer, :, k * d:(k + 1) * d].reshape(bsz, 1, d)

    x, h = _embed(x, emb_ln_g, emb_ln_b, mod(0, 1), mod(0, 0))
    for layer in range(depth):
        h2d = h.reshape(bsz * s, d)
        qkv = _qkv_proj(h2d, w_in[layer], 3 * aw).reshape(bsz, s, 3 * aw)
        u = _glu_proj(h2d, w_in[layer], 3 * aw, 3 * aw + cw, cw).reshape(bsz, s, cw)
        attn = _attention(qkv, bias, aw)
        uc = _conv_branch(u, conv_w[layer], conv_b[layer], conv_ln_g[layer], conv_ln_b[layer], conv_out_g[layer])
        x, h = _out_proj(attn, uc, x, w_out[layer], attn_out_g[layer], mod(layer, 2), ln1_g[layer],
                         ln1_b[layer], mod(layer, 4), mod(layer, 3), alpha)
        next_mod = (mod(layer + 1, 1), mod(layer + 1, 0)) if layer + 1 < depth else None
        x, h = _mlp(h, w_mlp1[layer], b_mlp1[layer], w_mlp2[layer], b_mlp2[layer], x, mod(layer, 5),
                    ln2_g[layer], ln2_b[layer], alpha, next_mod)
    return x


def kernel(x_prompt, x_sample, c_prompt, c_sample, emb_ln_g, emb_ln_b, w_ada, b_ada, w_in, conv_w, conv_b,
           conv_ln_g, conv_ln_b, attn_out_g, conv_out_g, w_out, ln1_g, ln1_b, w_mlp1, b_mlp1, w_mlp2, b_mlp2,
           ln2_g, ln2_b):
    d = x_prompt.shape[-1]
    n_prompt, n_sample = c_prompt.shape[0], c_sample.shape[0]
    pad = (-(n_prompt + n_sample)) % 8
    c_all = jnp.concatenate([c_prompt, c_sample, jnp.zeros((pad, d), F32)], axis=0)
    mods = _ada_mod(c_all, w_ada, b_ada)
    n_heads = (d // 2) // HEAD_DIM
    bias = _bias_tiles(n_heads)
    weights = (emb_ln_g, emb_ln_b, w_in.astype(BF16), conv_w, conv_b, conv_ln_g, conv_ln_b, attn_out_g,
               conv_out_g, w_out.astype(BF16), ln1_g, ln1_b, w_mlp1.astype(BF16), b_mlp1, w_mlp2.astype(BF16),
               b_mlp2, ln2_g, ln2_b, bias)
    y_prompt = _trunk(x_prompt, mods[:, :n_prompt], *weights)
    y_sample = _trunk(x_sample, mods[:, n_prompt:n_prompt + n_sample], *weights)
    return (y_prompt, y_sample)
```

```python
import functools

import jax
import jax.numpy as jnp
from jax import lax
from jax.experimental import pallas as pl
from jax.experimental.pallas import tpu as pltpu

F32 = jnp.float32
BF16 = jnp.bfloat16

HEAD_DIM = 64
DILATED_BRANCHES = ((128, 1), (512, 4), (2048, 16))
HALF_WINDOW = 64
CONV_KERNEL = 31
CONV_PAD = (CONV_KERNEL - 1) // 2
N_MOD = 6
LN_EPS = 1e-5
NEG_INF = -1e30

LANES = 128
SUBLANES = 8
VMEM_BYTES_V7X = 64 * 1024 * 1024
VMEM_CAP = VMEM_BYTES_V7X - 8 * 1024 * 1024

Q_BLOCK = 128
K_WINDOW = Q_BLOCK + 2 * HALF_WINDOW
ATTN_UNROLL = 8
ATTN_GATHER_ROWS = 256


def _nbytes(shape, dtype):
    n = 1
    for s in shape:
        n *= s
    return n * jnp.dtype(dtype).itemsize


def _vmem_limit(pipelined, scratch=0, temps=0):
    return int(min(VMEM_CAP, 2 * pipelined + scratch + temps + (2 << 20)))


def _tile(n, preferred, align=LANES):
    t = min(preferred, n) // align * align
    while n % t:
        t -= align
    return t


def _params(semantics, vmem):
    return pltpu.CompilerParams(dimension_semantics=semantics, vmem_limit_bytes=vmem)


def _layer_norm(y, g, b):
    mu = jnp.mean(y, axis=-1, keepdims=True)
    yc = y - mu
    var = jnp.mean(yc * yc, axis=-1, keepdims=True)
    return yc * lax.rsqrt(var + LN_EPS) * g + b


def _rms_norm(y, g):
    return y * lax.rsqrt(jnp.mean(y * y, axis=-1, keepdims=True) + LN_EPS) * g


def _dot(a, b):
    return jnp.dot(a, b, preferred_element_type=F32)


def _ada_kernel(c_ref, w_ref, b_ref, o_ref):
    c = c_ref[...]
    a = c * jax.nn.sigmoid(c)
    w = w_ref[0]
    a_hi = a.astype(BF16)
    a_lo = (a - a_hi.astype(F32)).astype(BF16)
    w_hi = w.astype(BF16)
    w_lo = (w - w_hi.astype(F32)).astype(BF16)
    o_ref[0] = _dot(a_hi, w_hi) + _dot(a_hi, w_lo) + _dot(a_lo, w_hi) + b_ref[0]


def _ada_mod(c, w_ada, b_ada):
    depth, d, n = w_ada.shape
    rows = c.shape[0]
    tn = _tile(n, 1024)
    blocks = _nbytes((rows, d), F32) + _nbytes((d, tn), F32) + 2 * _nbytes((8, tn), F32)
    return pl.pallas_call(
        _ada_kernel,
        grid=(depth, n // tn),
        in_specs=[
            pl.BlockSpec((rows, d), lambda l, j: (0, 0)),
            pl.BlockSpec((1, d, tn), lambda l, j: (l, 0, j)),
            pl.BlockSpec((1, 1, tn), lambda l, j: (l, 0, j)),
        ],
        out_specs=pl.BlockSpec((1, rows, tn), lambda l, j: (l, 0, j)),
        out_shape=jax.ShapeDtypeStruct((depth, rows, n), F32),
        compiler_params=_params(("parallel", "parallel"), _vmem_limit(blocks, temps=2 * _nbytes((d, tn), F32))),
        name="ada_mod",
    )(c, w_ada, b_ada.reshape(depth, 1, n))


def _embed_kernel(x_ref, g_ref, b_ref, sc_ref, sh_ref, x_out, h_out):
    x = _layer_norm(x_ref[0], g_ref[...], b_ref[...])
    x_out[0] = x
    h_out[0] = (x * (1.0 + sc_ref[0]) + sh_ref[0]).astype(BF16)


def _embed(x, g, b, sc, sh):
    bsz, s, d = x.shape
    ts = 256
    row = pl.BlockSpec((1, ts, d), lambda i, j: (i, j, 0))
    vec = pl.BlockSpec((1, d), lambda i, j: (0, 0))
    bvec = pl.BlockSpec((1, 1, d), lambda i, j: (i, 0, 0))
    blocks = 2 * _nbytes((ts, d), F32) + _nbytes((ts, d), BF16)
    return pl.pallas_call(
        _embed_kernel,
        grid=(bsz, s // ts),
        in_specs=[row, vec, vec, bvec, bvec],
        out_specs=[row, row],
        out_shape=[jax.ShapeDtypeStruct((bsz, s, d), F32), jax.ShapeDtypeStruct((bsz, s, d), BF16)],
        compiler_params=_params(("parallel", "parallel"), _vmem_limit(blocks, temps=4 * _nbytes((ts, d), F32))),
        name="embed_ln",
    )(x, g.reshape(1, d), b.reshape(1, d), sc, sh)


def _qkv_kernel(h_ref, w_ref, o_ref):
    o_ref[...] = _dot(h_ref[...], w_ref[...]).astype(o_ref.dtype)


def _qkv_proj(h2d, w_in, n_cols):
    m, d = h2d.shape
    tm, tn = _tile(m, 1024), _tile(n_cols, 1024)
    blocks = _nbytes((tm, d), BF16) + _nbytes((d, tn), BF16) + _nbytes((tm, tn), F32)
    return pl.pallas_call(
        _qkv_kernel,
        grid=(n_cols // tn, m // tm),
        in_specs=[
            pl.BlockSpec((tm, d), lambda n, i: (i, 0)),
            pl.BlockSpec((d, tn), lambda n, i: (0, n)),
        ],
        out_specs=pl.BlockSpec((tm, tn), lambda n, i: (i, n)),
        out_shape=jax.ShapeDtypeStruct((m, n_cols), F32),
        compiler_params=_params(("parallel", "parallel"), _vmem_limit(blocks, temps=2 * _nbytes((tm, tn), F32))),
        name="qkv_proj",
    )(h2d, w_in)


def _glu_kernel(h_ref, wv_ref, wg_ref, o_ref):
    h = h_ref[...]
    o_ref[...] = _dot(h, wv_ref[...]) * jax.nn.sigmoid(_dot(h, wg_ref[...]))


def _glu_proj(h2d, w_in, val_col, gate_col, width):
    m, d = h2d.shape
    tm, tn = _tile(m, 1024), _tile(width, 512)
    v_blk, g_blk = val_col // tn, gate_col // tn
    blocks = _nbytes((tm, d), BF16) + 2 * _nbytes((d, tn), BF16) + _nbytes((tm, tn), F32)
    return pl.pallas_call(
        _glu_kernel,
        grid=(width // tn, m // tm),
        in_specs=[
            pl.BlockSpec((tm, d), lambda n, i: (i, 0)),
            pl.BlockSpec((d, tn), lambda n, i: (0, v_blk + n)),
            pl.BlockSpec((d, tn), lambda n, i: (0, g_blk + n)),
        ],
        out_specs=pl.BlockSpec((tm, tn), lambda n, i: (i, n)),
        out_shape=jax.ShapeDtypeStruct((m, width), F32),
        compiler_params=_params(("parallel", "parallel"), _vmem_limit(blocks, temps=4 * _nbytes((tm, tn), F32))),
        name="glu_proj",
    )(h2d, w_in, w_in)


def _bias_tiles(n_heads):
    slopes = 2.0 ** (-8.0 * jnp.arange(1, n_heads + 1, dtype=F32) / n_heads)
    q = jnp.arange(Q_BLOCK)[:, None]
    c = jnp.arange(K_WINDOW)[None, :]
    branches = []
    for window, dilation in DILATED_BRANCHES:
        assert window == 2 * dilation * HALF_WINDOW
        tiles = []
        for key_start_minus_q_start in (0, -HALF_WINDOW, -2 * HALF_WINDOW):
            rel = key_start_minus_q_start + c - q
            dist = (dilation * jnp.abs(rel)).astype(F32)
            bias = -slopes[:, None, None] * dist[None]
            tiles.append(jnp.where((jnp.abs(rel) <= HALF_WINDOW)[None], bias, NEG_INF))
        branches.append(jnp.stack(tiles, axis=0))
    b = jnp.stack(branches, axis=0)
    b = b.reshape(len(DILATED_BRANCHES), 3, n_heads // 2, 2, Q_BLOCK, K_WINDOW)
    return b.transpose(2, 0, 1, 3, 4, 5)


def _attn_kernel(q_ref, k_ref, v_ref, bias_ref, o_ref, cq_ref, ck_ref, cv_ref, qs_ref, ks_ref, vs_ref,
                 acc_ref, m_ref, l_ref, stage_ref, *, seq):
    lane = lax.broadcasted_iota(jnp.int32, (1, LANES), 1)
    first_head = lane < HEAD_DIM
    scale = HEAD_DIM ** -0.5
    (_, near), (_, mid), (_, far) = DILATED_BRANCHES
    sub = far // mid
    len_mid, len_far = seq // mid, seq // far
    gather_mid_rows = min(ATTN_GATHER_ROWS, len_mid)
    gather_far_rows = min(ATTN_GATHER_ROWS, len_far)

    def window(i, length):
        q0 = pl.multiple_of(i * Q_BLOCK, Q_BLOCK)
        k0 = pl.multiple_of(jnp.clip(q0 - HALF_WINDOW, 0, length - K_WINDOW), HALF_WINDOW)
        return q0, k0, (q0 - k0) // HALF_WINDOW

    def partial_softmax(q, kw, vw, branch, kind):
        tops, dens, nums = [], [], []
        for head in range(2):
            mine = first_head if head == 0 else jnp.logical_not(first_head)
            qh = jnp.where(mine, q, jnp.zeros_like(q))
            s = lax.dot_general(qh, kw, (((1,), (1,)), ((), ())), preferred_element_type=F32)
            s = s + bias_ref[0, branch, kind, head]
            m = jnp.max(s, axis=-1, keepdims=True)
            p = jnp.exp(s - m)
            tops.append(m)
            dens.append(jnp.sum(p, axis=-1, keepdims=True))
            nums.append(_dot(p.astype(BF16), vw))
        return (jnp.where(first_head, tops[0], tops[1]), jnp.where(first_head, dens[0], dens[1]),
                jnp.where(first_head, nums[0], nums[1]))

    def merge(m_old, l_old, o_old, m_blk, l_blk, o_blk):
        m_new = jnp.maximum(m_old, m_blk)
        w_old = jnp.exp(m_old - m_new)
        w_blk = jnp.exp(m_blk - m_new)
        return m_new, w_old * l_old + w_blk * l_blk, w_old * o_old + w_blk * o_blk

    def grouped(n_blocks, block):
        unroll = _tile(n_blocks, ATTN_UNROLL, align=1)

        def group(g, carry):
            for u in range(unroll):
                block(g * unroll + u, u)
            return carry

        lax.fori_loop(0, n_blocks // unroll, group, 0)

    def mid_class(r, carry):
        base = pl.multiple_of(r * len_mid, len_mid)

        def gather_mid(c, carry2):
            c0 = pl.multiple_of(c * gather_mid_rows, gather_mid_rows)
            src = pl.ds(r + mid * c0, gather_mid_rows, stride=mid)
            dst = pl.ds(c0, gather_mid_rows)
            x = q_ref[0, src, :]
            cq_ref[dst, :] = x
            qs_ref[dst, :] = (x * scale).astype(BF16)
            x = k_ref[0, src, :]
            ck_ref[dst, :] = x
            ks_ref[dst, :] = x.astype(BF16)
            x = v_ref[0, src, :]
            cv_ref[dst, :] = x
            vs_ref[dst, :] = x.astype(BF16)
            return carry2

        lax.fori_loop(0, len_mid // gather_mid_rows, gather_mid, 0)

        def mid_block(i, slot):
            q0, k0, kind = window(i, len_mid)
            m, l, o = partial_softmax(qs_ref[pl.ds(q0, Q_BLOCK), :], ks_ref[pl.ds(k0, K_WINDOW), :],
                                      vs_ref[pl.ds(k0, K_WINDOW), :], 1, kind)
            rows = pl.ds(base + q0, Q_BLOCK)
            m_ref[rows, :] = m
            l_ref[rows, :] = l
            acc_ref[rows, :] = o

        grouped(len_mid // Q_BLOCK, mid_block)

        for c in range(sub):
            def gather_far(g, carry2, c=c):
                g0 = pl.multiple_of(g * gather_far_rows, gather_far_rows)
                src = pl.ds(c + sub * g0, gather_far_rows, stride=sub)
                dst = pl.ds(c * len_far + g0, gather_far_rows)
                qs_ref[dst, :] = (cq_ref[src, :] * scale).astype(BF16)
                ks_ref[dst, :] = ck_ref[src, :].astype(BF16)
                vs_ref[dst, :] = cv_ref[src, :].astype(BF16)
                return carry2

            lax.fori_loop(0, len_far // gather_far_rows, gather_far, 0)

        far_blocks = len_far // Q_BLOCK

        def far_block(j, slot):
            c = j // far_blocks
            q0, k0, kind = window(j - c * far_blocks, len_far)
            off = pl.multiple_of(c * len_far, len_far)
            m, l, o = partial_softmax(qs_ref[pl.ds(off + q0, Q_BLOCK), :], ks_ref[pl.ds(off + k0, K_WINDOW), :],
                                      vs_ref[pl.ds(off + k0, K_WINDOW), :], 2, kind)
            rows = pl.ds(base + c + sub * q0, Q_BLOCK, stride=sub)
            m, l, o = merge(m_ref[rows, :], l_ref[rows, :], acc_ref[rows, :], m, l, o)
            m_ref[rows, :] = m
            l_ref[rows, :] = l
            acc_ref[rows, :] = o

        grouped(sub * far_blocks, far_block)
        return carry

    lax.fori_loop(0, mid, mid_class, 0)

    per_class = Q_BLOCK // mid

    def near_block(i, slot):
        q0, k0, kind = window(i, seq)
        m, l, o = partial_softmax((q_ref[0, pl.ds(q0, Q_BLOCK), :] * scale).astype(BF16),
                                  k_ref[0, pl.ds(k0, K_WINDOW), :].astype(BF16),
                                  v_ref[0, pl.ds(k0, K_WINDOW), :].astype(BF16), 0, kind)
        j0 = pl.multiple_of(i * per_class, per_class)
        for c in range(mid):
            src = pl.ds(c * len_mid + j0, per_class)
            dst = pl.ds(c, per_class, stride=mid)
            stage_ref[slot, 0, dst, :] = m_ref[src, :]
            stage_ref[slot, 1, dst, :] = l_ref[src, :]
            stage_ref[slot, 2, dst, :] = acc_ref[src, :]
        m, l, o = merge(stage_ref[slot, 0], stage_ref[slot, 1], stage_ref[slot, 2], m, l, o)
        o_ref[0, pl.ds(q0, Q_BLOCK), :] = (o / l).astype(o_ref.dtype)

    grouped(seq // Q_BLOCK, near_block)


def _attention(qkv, bias, attn_width):
    bsz, seq, _ = qkv.shape
    n_pairs = attn_width // LANES
    (_, near), (_, mid), (_, far) = DILATED_BRANCHES
    assert near == 1 and far % mid == 0 and Q_BLOCK % mid == 0
    assert seq % (far * Q_BLOCK) == 0 and seq // far >= K_WINDOW

    def slab(col0):
        return pl.BlockSpec((1, seq, LANES), lambda p, b: (b, 0, col0 + p))

    bias_block = (1,) + bias.shape[1:]
    mid_rows = (seq // mid, LANES)
    stage = (ATTN_UNROLL, 3, Q_BLOCK, LANES)
    blocks = 3 * _nbytes((seq, LANES), F32) + _nbytes(bias_block, F32) + _nbytes((seq, LANES), BF16)
    scratch = (3 * _nbytes(mid_rows, F32) + 3 * _nbytes(mid_rows, BF16) + 3 * _nbytes((seq, LANES), F32)
               + _nbytes(stage, F32))
    return pl.pallas_call(
        functools.partial(_attn_kernel, seq=seq),
        grid=(n_pairs, bsz),
        in_specs=[slab(0), slab(n_pairs), slab(2 * n_pairs),
                  pl.BlockSpec(bias_block, lambda p, b: (p, 0, 0, 0, 0, 0))],
        out_specs=pl.BlockSpec((1, seq, LANES), lambda p, b: (b, 0, p)),
        out_shape=jax.ShapeDtypeStruct((bsz, seq, attn_width), BF16),
        scratch_shapes=([pltpu.VMEM(mid_rows, F32)] * 3 + [pltpu.VMEM(mid_rows, BF16)] * 3
                        + [pltpu.VMEM((seq, LANES), F32)] * 3 + [pltpu.VMEM(stage, F32)]),
        compiler_params=_params(("parallel", "parallel"),
                                _vmem_limit(blocks, scratch=scratch, temps=16 * _nbytes((Q_BLOCK, K_WINDOW), F32))),
        name="attention",
    )(qkv, qkv, qkv, bias)


CONV_HALO = 16
CONV_ROWS = 32
CONV_LANES = 256
CONV_NORM_ROWS = 64


def _conv_kernel(prev_ref, cur_ref, next_ref, w_ref, cb_ref, lg_ref, lb_ref, og_ref, o_ref, sh_ref, y_ref,
                 *, rows):
    j = pl.program_id(1)
    width = cur_ref.shape[-1]
    zeros = jnp.zeros((CONV_HALO, width), F32)
    sh_ref[0, 0:CONV_HALO, :] = jnp.where(j > 0, prev_ref[0], zeros)
    sh_ref[0, CONV_HALO:CONV_HALO + rows, :] = cur_ref[0]
    sh_ref[0, CONV_HALO + rows:, :] = jnp.where(j < pl.num_programs(1) - 1, next_ref[0], zeros)
    shifted_rows = rows + 2 * CONV_HALO - SUBLANES
    for b in range(1, SUBLANES):
        sh_ref[b, 0:shifted_rows, :] = sh_ref[0, b:b + shifted_rows, :]

    slab_rows = CONV_ROWS + 2 * CONV_HALO - SUBLANES

    def row_block(rb, carry):
        r0 = pl.multiple_of(rb * CONV_ROWS, CONV_ROWS)
        for c0 in range(0, width, CONV_LANES):
            acc = jnp.zeros((CONV_ROWS // SUBLANES, SUBLANES, CONV_LANES), F32)
            for shift in range(SUBLANES):
                slab = sh_ref[shift, pl.ds(r0, slab_rows), c0:c0 + CONV_LANES]
                slab = slab.reshape(slab_rows // SUBLANES, SUBLANES, CONV_LANES)
                for tap in range(CONV_KERNEL):
                    offset = tap + (CONV_HALO - CONV_PAD)
                    if offset % SUBLANES != shift:
                        continue
                    first = offset // SUBLANES
                    window = slab[first:first + CONV_ROWS // SUBLANES]
                    acc = acc + window * w_ref[tap, :, c0:c0 + CONV_LANES][None]
            y_ref[pl.ds(r0, CONV_ROWS), c0:c0 + CONV_LANES] = acc.reshape(CONV_ROWS, CONV_LANES)
        return carry

    lax.fori_loop(0, rows // CONV_ROWS, row_block, 0)

    for r0 in range(0, rows, CONV_NORM_ROWS):
        y = y_ref[r0:r0 + CONV_NORM_ROWS, :] + cb_ref[...]
        y = _layer_norm(y, lg_ref[...], lb_ref[...])
        y = y * jax.nn.sigmoid(y)
        o_ref[0, r0:r0 + CONV_NORM_ROWS, :] = _rms_norm(y, og_ref[...]).astype(o_ref.dtype)


def _conv_branch(u, conv_w, conv_b, ln_g, ln_b, out_g):
    bsz, s, width = u.shape
    rows = 256
    halo_blocks = rows // CONV_HALO
    last_halo = s // CONV_HALO - 1
    vec = pl.BlockSpec((1, width), lambda b, j: (0, 0))
    blocks = (2 * _nbytes((CONV_HALO, width), F32) + _nbytes((rows, width), F32)
              + _nbytes((CONV_KERNEL, SUBLANES, width), F32) + _nbytes((rows, width), BF16))
    shifted = (SUBLANES, rows + 2 * CONV_HALO, width)
    scratch = _nbytes(shifted, F32) + _nbytes((rows, width), F32)
    w_tiles = jnp.broadcast_to(conv_w[:, None, :], (CONV_KERNEL, SUBLANES, width))
    return pl.pallas_call(
        functools.partial(_conv_kernel, rows=rows),
        grid=(bsz, s // rows),
        in_specs=[
            pl.BlockSpec((1, CONV_HALO, width), lambda b, j: (b, jnp.maximum(j * halo_blocks - 1, 0), 0)),
            pl.BlockSpec((1, rows, width), lambda b, j: (b, j, 0)),
            pl.BlockSpec((1, CONV_HALO, width), lambda b, j: (b, jnp.minimum((j + 1) * halo_blocks, last_halo), 0)),
            pl.BlockSpec((CONV_KERNEL, SUBLANES, width), lambda b, j: (0, 0, 0)),
            vec, vec, vec, vec,
        ],
        out_specs=pl.BlockSpec((1, rows, width), lambda b, j: (b, j, 0)),
        out_shape=jax.ShapeDtypeStruct((bsz, s, width), BF16),
        scratch_shapes=[pltpu.VMEM(shifted, F32), pltpu.VMEM((rows, width), F32)],
        compiler_params=_params(("parallel", "parallel"),
                                _vmem_limit(blocks, scratch=scratch, temps=6 * _nbytes((rows, width), F32))),
        name="conv_branch",
    )(u, u, u, w_tiles, conv_b.reshape(1, width), ln_g.reshape(1, width), ln_b.reshape(1, width),
      out_g.reshape(1, width))


OUT_SUBTILES = 2


def _out_kernel(at_ref, uc_ref, x_ref, w_ref, ag_ref, gate_ref, lg_ref, lb_ref, sc_ref, sh_ref, x_out, h_out,
                *, alpha):
    sub = at_ref.shape[1] // OUT_SUBTILES
    for r0 in range(0, at_ref.shape[1], sub):
        rows = pl.ds(r0, sub)
        attn = _rms_norm(at_ref[0, rows, :].astype(F32), ag_ref[...]).astype(BF16)
        mix = _dot(jnp.concatenate([attn, uc_ref[0, rows, :]], axis=-1), w_ref[...])
        x = _layer_norm(alpha * x_ref[0, rows, :] + (1.0 + gate_ref[0]) * mix, lg_ref[...], lb_ref[...])
        x_out[0, rows, :] = x
        h_out[0, rows, :] = (x * (1.0 + sc_ref[0]) + sh_ref[0]).astype(BF16)


def _out_proj(attn, uc, x, w_out, attn_g, gate, ln_g, ln_b, sc, sh, alpha):
    bsz, s, d = x.shape
    aw = attn.shape[-1]
    cw = uc.shape[-1]
    tm = _tile(s, 512)

    def rows(width):
        return pl.BlockSpec((1, tm, width), lambda b, i: (b, i, 0))

    def vec(width):
        return pl.BlockSpec((1, width), lambda b, i: (0, 0))

    bvec = pl.BlockSpec((1, 1, d), lambda b, i: (b, 0, 0))
    blocks = (_nbytes((tm, aw), BF16) + _nbytes((tm, cw), BF16) + 2 * _nbytes((tm, d), F32)
              + _nbytes((tm, d), BF16) + _nbytes((d, d), BF16))
    return pl.pallas_call(
        functools.partial(_out_kernel, alpha=alpha),
        grid=(bsz, s // tm),
        in_specs=[rows(aw), rows(cw), rows(d), pl.BlockSpec((d, d), lambda b, i: (0, 0)), vec(aw), bvec,
                  vec(d), vec(d), bvec, bvec],
        out_specs=[rows(d), rows(d)],
        out_shape=[jax.ShapeDtypeStruct((bsz, s, d), F32), jax.ShapeDtypeStruct((bsz, s, d), BF16)],
        compiler_params=_params(("parallel", "parallel"), _vmem_limit(blocks, temps=4 * _nbytes((tm, d), F32))),
        name="out_proj",
    )(attn, uc, x, w_out, attn_g.reshape(1, aw), gate, ln_g.reshape(1, d), ln_b.reshape(1, d), sc, sh)


def _mlp_kernel(h_ref, w1_ref, b1_ref, w2_ref, b2_ref, x_ref, gate_ref, lg_ref, lb_ref, *rest, alpha, modulate):
    if modulate:
        sc_ref, sh_ref, x_out, h_out, acc_ref = rest
    else:
        x_out, acc_ref = rest
    f = pl.program_id(2)

    @pl.when(f == 0)
    def _():
        acc_ref[...] = jnp.zeros_like(acc_ref)

    hid = jnp.maximum(_dot(h_ref[0], w1_ref[...]) + b1_ref[...], 0.0)
    acc_ref[...] += _dot((hid * hid).astype(BF16), w2_ref[...])

    @pl.when(f == pl.num_programs(2) - 1)
    def _():
        ff = acc_ref[...] + b2_ref[...]
        x = _layer_norm(alpha * x_ref[0] + (1.0 + gate_ref[0]) * ff, lg_ref[...], lb_ref[...])
        x_out[0] = x
        if modulate:
            h_out[0] = (x * (1.0 + sc_ref[0]) + sh_ref[0]).astype(BF16)


def _mlp(h, w1, b1, w2, b2, x, gate, ln_g, ln_b, alpha, next_mod=None):
    bsz, s, d = x.shape
    ff_dim = w1.shape[1]
    tm, tf = _tile(s, 512), _tile(ff_dim, 1024)
    modulate = next_mod is not None
    row_f32 = pl.BlockSpec((1, tm, d), lambda b, i, f: (b, i, 0))
    vec_d = pl.BlockSpec((1, d), lambda b, i, f: (0, 0))
    bvec = pl.BlockSpec((1, 1, d), lambda b, i, f: (b, 0, 0))
    in_specs = [
        row_f32,
        pl.BlockSpec((d, tf), lambda b, i, f: (0, f)),
        pl.BlockSpec((1, tf), lambda b, i, f: (0, f)),
        pl.BlockSpec((tf, d), lambda b, i, f: (f, 0)),
        vec_d, row_f32, bvec, vec_d, vec_d,
    ]
    args = [h, w1, b1.reshape(1, ff_dim), w2, b2.reshape(1, d), x, gate, ln_g.reshape(1, d), ln_b.reshape(1, d)]
    out_specs = [row_f32]
    out_shape = [jax.ShapeDtypeStruct((bsz, s, d), F32)]
    blocks = (_nbytes((tm, d), BF16) + 2 * _nbytes((d, tf), BF16) + 2 * _nbytes((tm, d), F32))
    if modulate:
        in_specs += [bvec, bvec]
        args += list(next_mod)
        out_specs.append(row_f32)
        out_shape.append(jax.ShapeDtypeStruct((bsz, s, d), BF16))
        blocks += _nbytes((tm, d), BF16)
    out = pl.pallas_call(
        functools.partial(_mlp_kernel, alpha=alpha, modulate=modulate),
        grid=(bsz, s // tm, ff_dim // tf),
        in_specs=in_specs,
        out_specs=out_specs,
        out_shape=out_shape,
        scratch_shapes=[pltpu.VMEM((tm, d), F32)],
        compiler_params=_params(("parallel", "parallel", "arbitrary"),
                                _vmem_limit(blocks, scratch=_nbytes((tm, d), F32), temps=3 * _nbytes((tm, tf), F32))),
        name="mlp",
    )(*args)
    return out if modulate else (out[0], None)


def _trunk(x, mods, emb_ln_g, emb_ln_b, w_in, conv_w, conv_b, conv_ln_g, conv_ln_b, attn_out_g, conv_out_g,
           w_out, ln1_g, ln1_b, w_mlp1, b_mlp1, w_mlp2, b_mlp2, ln2_g, ln2_b, bias):
    bsz, s, d = x.shape
    depth = w_in.shape[0]
    aw = d // 2
    cw = d - aw
    alpha = (2 * depth) ** 0.25

    def mod(layer, k):
        return mods[layer, :, k * d:(k + 1) * d].reshape(bsz, 1, d)

    x, h = _embed(x, emb_ln_g, emb_ln_b, mod(0, 1), mod(0, 0))
    for layer in range(depth):
        h2d = h.reshape(bsz * s, d)
        qkv = _qkv_proj(h2d, w_in[layer], 3 * aw).reshape(bsz, s, 3 * aw)
        u = _glu_proj(h2d, w_in[layer], 3 * aw, 3 * aw + cw, cw).reshape(bsz, s, cw)
        attn = _attention(qkv, bias, aw)
        uc = _conv_branch(u, conv_w[layer], conv_b[layer], conv_ln_g[layer], conv_ln_b[layer], conv_out_g[layer])
        x, h = _out_proj(attn, uc, x, w_out[layer], attn_out_g[layer], mod(layer, 2), ln1_g[layer],
                         ln1_b[layer], mod(layer, 4), mod(layer, 3), alpha)
        next_mod = (mod(layer + 1, 1), mod(layer + 1, 0)) if layer + 1 < depth else None
        x, h = _mlp(h, w_mlp1[layer], b_mlp1[layer], w_mlp2[layer], b_mlp2[layer], x, mod(layer, 5),
                    ln2_g[layer], ln2_b[layer], alpha, next_mod)
    return x


def kernel(x_prompt, x_sample, c_prompt, c_sample, emb_ln_g, emb_ln_b, w_ada, b_ada, w_in, conv_w, conv_b,
           conv_ln_g, conv_ln_b, attn_out_g, conv_out_g, w_out, ln1_g, ln1_b, w_mlp1, b_mlp1, w_mlp2, b_mlp2,
           ln2_g, ln2_b):
    d = x_prompt.shape[-1]
    n_prompt, n_sample = c_prompt.shape[0], c_sample.shape[0]
    pad = (-(n_prompt + n_sample)) % 8
    c_all = jnp.concatenate([c_prompt, c_sample, jnp.zeros((pad, d), F32)], axis=0)
    mods = _ada_mod(c_all, w_ada, b_ada)
    n_heads = (d // 2) // HEAD_DIM
    bias = _bias_tiles(n_heads)
    weights = (emb_ln_g, emb_ln_b, w_in.astype(BF16), conv_w, conv_b, conv_ln_g, conv_ln_b, attn_out_g,
               conv_out_g, w_out.astype(BF16), ln1_g, ln1_b, w_mlp1.astype(BF16), b_mlp1, w_mlp2.astype(BF16),
               b_mlp2, ln2_g, ln2_b, bias)
    y_prompt = _trunk(x_prompt, mods[:, :n_prompt], *weights)
    y_sample = _trunk(x_sample, mods[:, n_prompt:n_prompt + n_sample], *weights)
    return (y_prompt, y_sample)
```

```python
import functools

import jax
import jax.numpy as jnp
from jax import lax
from jax.experimental import pallas as pl
from jax.experimental.pallas import tpu as pltpu

F32 = jnp.float32
BF16 = jnp.bfloat16

HEAD_DIM = 64
DILATED_BRANCHES = ((128, 1), (512, 4), (2048, 16))
HALF_WINDOW = 64
CONV_KERNEL = 31
CONV_PAD = (CONV_KERNEL - 1) // 2
N_MOD = 6
LN_EPS = 1e-5
NEG_INF = -1e30

LANES = 128
SUBLANES = 8
VMEM_BYTES_V7X = 64 * 1024 * 1024
VMEM_CAP = VMEM_BYTES_V7X - 8 * 1024 * 1024

Q_BLOCK = 128
K_WINDOW = Q_BLOCK + 2 * HALF_WINDOW
ATTN_UNROLL = 16
ATTN_GATHER_ROWS = 256


def _nbytes(shape, dtype):
    n = 1
    for s in shape:
        n *= s
    return n * jnp.dtype(dtype).itemsize


def _vmem_limit(pipelined, scratch=0, temps=0):
    return int(min(VMEM_CAP, 2 * pipelined + scratch + temps + (2 << 20)))


def _tile(n, preferred, align=LANES):
    t = min(preferred, n) // align * align
    while n % t:
        t -= align
    return t


def _params(semantics, vmem):
    return pltpu.CompilerParams(dimension_semantics=semantics, vmem_limit_bytes=vmem)


def _layer_norm(y, g, b):
    mu = jnp.mean(y, axis=-1, keepdims=True)
    yc = y - mu
    var = jnp.mean(yc * yc, axis=-1, keepdims=True)
    return yc * lax.rsqrt(var + LN_EPS) * g + b


def _rms_norm(y, g):
    return y * lax.rsqrt(jnp.mean(y * y, axis=-1, keepdims=True) + LN_EPS) * g


def _dot(a, b):
    return jnp.dot(a, b, preferred_element_type=F32)


def _ada_kernel(c_ref, w_ref, b_ref, o_ref):
    c = c_ref[...]
    a = c * jax.nn.sigmoid(c)
    w = w_ref[0]
    a_hi = a.astype(BF16)
    a_lo = (a - a_hi.astype(F32)).astype(BF16)
    w_hi = w.astype(BF16)
    w_lo = (w - w_hi.astype(F32)).astype(BF16)
    o_ref[0] = _dot(a_hi, w_hi) + _dot(a_hi, w_lo) + _dot(a_lo, w_hi) + b_ref[0]


def _ada_mod(c, w_ada, b_ada):
    depth, d, n = w_ada.shape
    rows = c.shape[0]
    tn = _tile(n, 1024)
    blocks = _nbytes((rows, d), F32) + _nbytes((d, tn), F32) + 2 * _nbytes((8, tn), F32)
    return pl.pallas_call(
        _ada_kernel,
        grid=(depth, n // tn),
        in_specs=[
            pl.BlockSpec((rows, d), lambda l, j: (0, 0)),
            pl.BlockSpec((1, d, tn), lambda l, j: (l, 0, j)),
            pl.BlockSpec((1, 1, tn), lambda l, j: (l, 0, j)),
        ],
        out_specs=pl.BlockSpec((1, rows, tn), lambda l, j: (l, 0, j)),
        out_shape=jax.ShapeDtypeStruct((depth, rows, n), F32),
        compiler_params=_params(("parallel", "parallel"), _vmem_limit(blocks, temps=2 * _nbytes((d, tn), F32))),
        name="ada_mod",
    )(c, w_ada, b_ada.reshape(depth, 1, n))


def _embed_kernel(x_ref, g_ref, b_ref, sc_ref, sh_ref, x_out, h_out):
    x = _layer_norm(x_ref[0], g_ref[...], b_ref[...])
    x_out[0] = x
    h_out[0] = (x * (1.0 + sc_ref[0]) + sh_ref[0]).astype(BF16)


def _embed(x, g, b, sc, sh):
    bsz, s, d = x.shape
    ts = 256
    row = pl.BlockSpec((1, ts, d), lambda i, j: (i, j, 0))
    vec = pl.BlockSpec((1, d), lambda i, j: (0, 0))
    bvec = pl.BlockSpec((1, 1, d), lambda i, j: (i, 0, 0))
    blocks = 2 * _nbytes((ts, d), F32) + _nbytes((ts, d), BF16)
    return pl.pallas_call(
        _embed_kernel,
        grid=(bsz, s // ts),
        in_specs=[row, vec, vec, bvec, bvec],
        out_specs=[row, row],
        out_shape=[jax.ShapeDtypeStruct((bsz, s, d), F32), jax.ShapeDtypeStruct((bsz, s, d), BF16)],
        compiler_params=_params(("parallel", "parallel"), _vmem_limit(blocks, temps=4 * _nbytes((ts, d), F32))),
        name="embed_ln",
    )(x, g.reshape(1, d), b.reshape(1, d), sc, sh)


def _qkv_kernel(h_ref, w_ref, o_ref):
    o_ref[...] = _dot(h_ref[...], w_ref[...]).astype(o_ref.dtype)


def _qkv_proj(h2d, w_in, layer, n_cols):
    m, d = h2d.shape
    tm, tn = _tile(m, 1024), _tile(n_cols, 1024)
    blocks = _nbytes((tm, d), BF16) + _nbytes((d, tn), BF16) + _nbytes((tm, tn), F32)
    return pl.pallas_call(
        _qkv_kernel,
        grid=(n_cols // tn, m // tm),
        in_specs=[
            pl.BlockSpec((tm, d), lambda n, i: (i, 0)),
            pl.BlockSpec((None, d, tn), lambda n, i: (layer, 0, n)),
        ],
        out_specs=pl.BlockSpec((tm, tn), lambda n, i: (i, n)),
        out_shape=jax.ShapeDtypeStruct((m, n_cols), F32),
        compiler_params=_params(("parallel", "parallel"), _vmem_limit(blocks, temps=2 * _nbytes((tm, tn), F32))),
        name="qkv_proj",
    )(h2d, w_in)


def _glu_kernel(h_ref, wv_ref, wg_ref, o_ref):
    h = h_ref[...]
    o_ref[...] = _dot(h, wv_ref[...]) * jax.nn.sigmoid(_dot(h, wg_ref[...]))


def _glu_proj(h2d, w_in, layer, val_col, gate_col, width):
    m, d = h2d.shape
    tm, tn = _tile(m, 1024), _tile(width, 512)
    v_blk, g_blk = val_col // tn, gate_col // tn
    blocks = _nbytes((tm, d), BF16) + 2 * _nbytes((d, tn), BF16) + _nbytes((tm, tn), F32)
    return pl.pallas_call(
        _glu_kernel,
        grid=(width // tn, m // tm),
        in_specs=[
            pl.BlockSpec((tm, d), lambda n, i: (i, 0)),
            pl.BlockSpec((None, d, tn), lambda n, i: (layer, 0, v_blk + n)),
            pl.BlockSpec((None, d, tn), lambda n, i: (layer, 0, g_blk + n)),
        ],
        out_specs=pl.BlockSpec((tm, tn), lambda n, i: (i, n)),
        out_shape=jax.ShapeDtypeStruct((m, width), F32),
        compiler_params=_params(("parallel", "parallel"), _vmem_limit(blocks, temps=4 * _nbytes((tm, tn), F32))),
        name="glu_proj",
    )(h2d, w_in, w_in)


def _bias_tiles(n_heads):
    slopes = 2.0 ** (-8.0 * jnp.arange(1, n_heads + 1, dtype=F32) / n_heads)
    q = jnp.arange(Q_BLOCK)[:, None]
    c = jnp.arange(K_WINDOW)[None, :]
    branches = []
    for window, dilation in DILATED_BRANCHES:
        assert window == 2 * dilation * HALF_WINDOW
        tiles = []
        for key_start_minus_q_start in (0, -HALF_WINDOW, -2 * HALF_WINDOW):
            rel = key_start_minus_q_start + c - q
            dist = (dilation * jnp.abs(rel)).astype(F32)
            bias = -slopes[:, None, None] * dist[None]
            tiles.append(jnp.where((jnp.abs(rel) <= HALF_WINDOW)[None], bias, NEG_INF))
        branches.append(jnp.stack(tiles, axis=0))
    b = jnp.stack(branches, axis=0)
    b = b.reshape(len(DILATED_BRANCHES), 3, n_heads // 2, 2, Q_BLOCK, K_WINDOW)
    return b.transpose(2, 0, 1, 3, 4, 5)


def _attn_kernel(q_ref, k_ref, v_ref, bias_ref, o_ref, cq_ref, ck_ref, cv_ref, qs_ref, ks_ref, vs_ref,
                 acc_ref, m_ref, l_ref, stage_ref, *, seq):
    lane = lax.broadcasted_iota(jnp.int32, (1, LANES), 1)
    first_head = lane < HEAD_DIM
    scale = HEAD_DIM ** -0.5
    (_, near), (_, mid), (_, far) = DILATED_BRANCHES
    sub = far // mid
    len_mid, len_far = seq // mid, seq // far
    gather_mid_rows = min(ATTN_GATHER_ROWS, len_mid)
    gather_far_rows = min(ATTN_GATHER_ROWS, len_far)

    def window(i, length):
        q0 = pl.multiple_of(i * Q_BLOCK, Q_BLOCK)
        k0 = pl.multiple_of(jnp.clip(q0 - HALF_WINDOW, 0, length - K_WINDOW), HALF_WINDOW)
        return q0, k0, (q0 - k0) // HALF_WINDOW

    def partial_softmax(q, kw, vw, branch, kind):
        tops, dens, nums = [], [], []
        for head in range(2):
            mine = first_head if head == 0 else jnp.logical_not(first_head)
            qh = jnp.where(mine, q, jnp.zeros_like(q))
            s = lax.dot_general(qh, kw, (((1,), (1,)), ((), ())), preferred_element_type=F32)
            s = s + bias_ref[0, branch, kind, head]
            m = jnp.max(s, axis=-1, keepdims=True)
            p = jnp.exp(s - m)
            tops.append(m)
            dens.append(jnp.sum(p, axis=-1, keepdims=True))
            nums.append(_dot(p.astype(BF16), vw))
        return (jnp.where(first_head, tops[0], tops[1]), jnp.where(first_head, dens[0], dens[1]),
                jnp.where(first_head, nums[0], nums[1]))

    def merge(m_old, l_old, o_old, m_blk, l_blk, o_blk):
        m_new = jnp.maximum(m_old, m_blk)
        w_old = jnp.exp(m_old - m_new)
        w_blk = jnp.exp(m_blk - m_new)
        return m_new, w_old * l_old + w_blk * l_blk, w_old * o_old + w_blk * o_blk

    def grouped(n_blocks, block):
        unroll = _tile(n_blocks, ATTN_UNROLL, align=1)

        def group(g, carry):
            for u in range(unroll):
                block(g * unroll + u, u)
            return carry

        lax.fori_loop(0, n_blocks // unroll, group, 0)

    def mid_class(r, carry):
        base = pl.multiple_of(r * len_mid, len_mid)

        def gather_mid(c, carry2):
            c0 = pl.multiple_of(c * gather_mid_rows, gather_mid_rows)
            src = pl.ds(r + mid * c0, gather_mid_rows, stride=mid)
            dst = pl.ds(c0, gather_mid_rows)
            x = q_ref[0, src, :]
            cq_ref[dst, :] = x
            qs_ref[dst, :] = (x * scale).astype(BF16)
            x = k_ref[0, src, :]
            ck_ref[dst, :] = x
            ks_ref[dst, :] = x.astype(BF16)
            x = v_ref[0, src, :]
            cv_ref[dst, :] = x
            vs_ref[dst, :] = x.astype(BF16)
            return carry2

        lax.fori_loop(0, len_mid // gather_mid_rows, gather_mid, 0)

        def mid_block(i, slot):
            q0, k0, kind = window(i, len_mid)
            m, l, o = partial_softmax(qs_ref[pl.ds(q0, Q_BLOCK), :], ks_ref[pl.ds(k0, K_WINDOW), :],
                                      vs_ref[pl.ds(k0, K_WINDOW), :], 1, kind)
            rows = pl.ds(base + q0, Q_BLOCK)
            m_ref[rows, :] = m
            l_ref[rows, :] = l
            acc_ref[rows, :] = o

        grouped(len_mid // Q_BLOCK, mid_block)

        for c in range(sub):
            def gather_far(g, carry2, c=c):
                g0 = pl.multiple_of(g * gather_far_rows, gather_far_rows)
                src = pl.ds(c + sub * g0, gather_far_rows, stride=sub)
                dst = pl.ds(c * len_far + g0, gather_far_rows)
                qs_ref[dst, :] = (cq_ref[src, :] * scale).astype(BF16)
                ks_ref[dst, :] = ck_ref[src, :].astype(BF16)
                vs_ref[dst, :] = cv_ref[src, :].astype(BF16)
                return carry2

            lax.fori_loop(0, len_far // gather_far_rows, gather_far, 0)

        far_blocks = len_far // Q_BLOCK

        def far_block(j, slot):
            c = j // far_blocks
            q0, k0, kind = window(j - c * far_blocks, len_far)
            off = pl.multiple_of(c * len_far, len_far)
            m, l, o = partial_softmax(qs_ref[pl.ds(off + q0, Q_BLOCK), :], ks_ref[pl.ds(off + k0, K_WINDOW), :],
                                      vs_ref[pl.ds(off + k0, K_WINDOW), :], 2, kind)
            rows = pl.ds(base + c + sub * q0, Q_BLOCK, stride=sub)
            m, l, o = merge(m_ref[rows, :], l_ref[rows, :], acc_ref[rows, :], m, l, o)
            m_ref[rows, :] = m
            l_ref[rows, :] = l
            acc_ref[rows, :] = o

        grouped(sub * far_blocks, far_block)
        return carry

    lax.fori_loop(0, mid, mid_class, 0)

    per_class = Q_BLOCK // mid

    def near_block(i, slot):
        q0, k0, kind = window(i, seq)
        m, l, o = partial_softmax((q_ref[0, pl.ds(q0, Q_BLOCK), :] * scale).astype(BF16),
                                  k_ref[0, pl.ds(k0, K_WINDOW), :].astype(BF16),
                                  v_ref[0, pl.ds(k0, K_WINDOW), :].astype(BF16), 0, kind)
        j0 = pl.multiple_of(i * per_class, per_class)
        for c in range(mid):
            src = pl.ds(c * len_mid + j0, per_class)
            dst = pl.ds(c, per_class, stride=mid)
            stage_ref[slot, 0, dst, :] = m_ref[src, :]
            stage_ref[slot, 1, dst, :] = l_ref[src, :]
            stage_ref[slot, 2, dst, :] = acc_ref[src, :]
        m, l, o = merge(stage_ref[slot, 0], stage_ref[slot, 1], stage_ref[slot, 2], m, l, o)
        o_ref[0, pl.ds(q0, Q_BLOCK), :] = (o / l).astype(o_ref.dtype)

    grouped(seq // Q_BLOCK, near_block)


def _attention(qkv, bias, attn_width):
    bsz, seq, _ = qkv.shape
    n_pairs = attn_width // LANES
    (_, near), (_, mid), (_, far) = DILATED_BRANCHES
    assert near == 1 and far % mid == 0 and Q_BLOCK % mid == 0
    assert seq % (far * Q_BLOCK) == 0 and seq // far >= K_WINDOW

    def slab(col0):
        return pl.BlockSpec((1, seq, LANES), lambda p, b: (b, 0, col0 + p))

    bias_block = (1,) + bias.shape[1:]
    mid_rows = (seq // mid, LANES)
    stage = (ATTN_UNROLL, 3, Q_BLOCK, LANES)
    blocks = 3 * _nbytes((seq, LANES), F32) + _nbytes(bias_block, F32) + _nbytes((seq, LANES), BF16)
    scratch = (3 * _nbytes(mid_rows, F32) + 3 * _nbytes(mid_rows, BF16) + 3 * _nbytes((seq, LANES), F32)
               + _nbytes(stage, F32))
    return pl.pallas_call(
        functools.partial(_attn_kernel, seq=seq),
        grid=(n_pairs, bsz),
        in_specs=[slab(0), slab(n_pairs), slab(2 * n_pairs),
                  pl.BlockSpec(bias_block, lambda p, b: (p, 0, 0, 0, 0, 0))],
        out_specs=pl.BlockSpec((1, seq, LANES), lambda p, b: (b, 0, p)),
        out_shape=jax.ShapeDtypeStruct((bsz, seq, attn_width), BF16),
        scratch_shapes=([pltpu.VMEM(mid_rows, F32)] * 3 + [pltpu.VMEM(mid_rows, BF16)] * 3
                        + [pltpu.VMEM((seq, LANES), F32)] * 3 + [pltpu.VMEM(stage, F32)]),
        compiler_params=_params(("parallel", "parallel"),
                                _vmem_limit(blocks, scratch=scratch, temps=16 * _nbytes((Q_BLOCK, K_WINDOW), F32))),
        name="attention",
    )(qkv, qkv, qkv, bias)


CONV_HALO = 16
CONV_ROWS = 32
CONV_LANES = 256
CONV_NORM_ROWS = 64


def _conv_kernel(prev_ref, cur_ref, next_ref, w_ref, cb_ref, lg_ref, lb_ref, og_ref, o_ref, sh_ref, y_ref,
                 *, rows):
    j = pl.program_id(1)
    width = cur_ref.shape[-1]
    zeros = jnp.zeros((CONV_HALO, width), F32)
    sh_ref[0, 0:CONV_HALO, :] = jnp.where(j > 0, prev_ref[0], zeros)
    sh_ref[0, CONV_HALO:CONV_HALO + rows, :] = cur_ref[0]
    sh_ref[0, CONV_HALO + rows:, :] = jnp.where(j < pl.num_programs(1) - 1, next_ref[0], zeros)
    shifted_rows = rows + 2 * CONV_HALO - SUBLANES
    for b in range(1, SUBLANES):
        sh_ref[b, 0:shifted_rows, :] = sh_ref[0, b:b + shifted_rows, :]

    slab_rows = CONV_ROWS + 2 * CONV_HALO - SUBLANES

    def row_block(rb, carry):
        r0 = pl.multiple_of(rb * CONV_ROWS, CONV_ROWS)
        for c0 in range(0, width, CONV_LANES):
            acc = jnp.zeros((CONV_ROWS // SUBLANES, SUBLANES, CONV_LANES), F32)
            for shift in range(SUBLANES):
                slab = sh_ref[shift, pl.ds(r0, slab_rows), c0:c0 + CONV_LANES]
                slab = slab.reshape(slab_rows // SUBLANES, SUBLANES, CONV_LANES)
                for tap in range(CONV_KERNEL):
                    offset = tap + (CONV_HALO - CONV_PAD)
                    if offset % SUBLANES != shift:
                        continue
                    first = offset // SUBLANES
                    window = slab[first:first + CONV_ROWS // SUBLANES]
                    acc = acc + window * w_ref[tap, :, c0:c0 + CONV_LANES][None]
            y_ref[pl.ds(r0, CONV_ROWS), c0:c0 + CONV_LANES] = acc.reshape(CONV_ROWS, CONV_LANES)
        return carry

    lax.fori_loop(0, rows // CONV_ROWS, row_block, 0)

    for r0 in range(0, rows, CONV_NORM_ROWS):
        y = y_ref[r0:r0 + CONV_NORM_ROWS, :] + cb_ref[...]
        y = _layer_norm(y, lg_ref[...], lb_ref[...])
        y = y * jax.nn.sigmoid(y)
        o_ref[0, r0:r0 + CONV_NORM_ROWS, :] = _rms_norm(y, og_ref[...]).astype(o_ref.dtype)


def _conv_branch(u, conv_w, conv_b, ln_g, ln_b, out_g):
    bsz, s, width = u.shape
    rows = 256
    halo_blocks = rows // CONV_HALO
    last_halo = s // CONV_HALO - 1
    vec = pl.BlockSpec((1, width), lambda b, j: (0, 0))
    blocks = (2 * _nbytes((CONV_HALO, width), F32) + _nbytes((rows, width), F32)
              + _nbytes((CONV_KERNEL, SUBLANES, width), F32) + _nbytes((rows, width), BF16))
    shifted = (SUBLANES, rows + 2 * CONV_HALO, width)
    scratch = _nbytes(shifted, F32) + _nbytes((rows, width), F32)
    w_tiles = jnp.broadcast_to(conv_w[:, None, :], (CONV_KERNEL, SUBLANES, width))
    return pl.pallas_call(
        functools.partial(_conv_kernel, rows=rows),
        grid=(bsz, s // rows),
        in_specs=[
            pl.BlockSpec((1, CONV_HALO, width), lambda b, j: (b, jnp.maximum(j * halo_blocks - 1, 0), 0)),
            pl.BlockSpec((1, rows, width), lambda b, j: (b, j, 0)),
            pl.BlockSpec((1, CONV_HALO, width), lambda b, j: (b, jnp.minimum((j + 1) * halo_blocks, last_halo), 0)),
            pl.BlockSpec((CONV_KERNEL, SUBLANES, width), lambda b, j: (0, 0, 0)),
            vec, vec, vec, vec,
        ],
        out_specs=pl.BlockSpec((1, rows, width), lambda b, j: (b, j, 0)),
        out_shape=jax.ShapeDtypeStruct((bsz, s, width), BF16),
        scratch_shapes=[pltpu.VMEM(shifted, F32), pltpu.VMEM((rows, width), F32)],
        compiler_params=_params(("parallel", "parallel"),
                                _vmem_limit(blocks, scratch=scratch, temps=6 * _nbytes((rows, width), F32))),
        name="conv_branch",
    )(u, u, u, w_tiles, conv_b.reshape(1, width), ln_g.reshape(1, width), ln_b.reshape(1, width),
      out_g.reshape(1, width))


OUT_SUBTILES = 2


def _out_kernel(at_ref, uc_ref, x_ref, w_ref, ag_ref, gate_ref, lg_ref, lb_ref, sc_ref, sh_ref, x_out, h_out,
                *, alpha):
    sub = at_ref.shape[1] // OUT_SUBTILES
    for r0 in range(0, at_ref.shape[1], sub):
        rows = pl.ds(r0, sub)
        attn = _rms_norm(at_ref[0, rows, :].astype(F32), ag_ref[...]).astype(BF16)
        mix = _dot(jnp.concatenate([attn, uc_ref[0, rows, :]], axis=-1), w_ref[...])
        x = _layer_norm(alpha * x_ref[0, rows, :] + (1.0 + gate_ref[0]) * mix, lg_ref[...], lb_ref[...])
        x_out[0, rows, :] = x
        h_out[0, rows, :] = (x * (1.0 + sc_ref[0]) + sh_ref[0]).astype(BF16)


def _out_proj(attn, uc, x, w_out, layer, attn_g, gate, ln_g, ln_b, sc, sh, alpha):
    bsz, s, d = x.shape
    aw = attn.shape[-1]
    cw = uc.shape[-1]
    tm = _tile(s, 512)

    def rows(width):
        return pl.BlockSpec((1, tm, width), lambda b, i: (b, i, 0))

    def vec(width):
        return pl.BlockSpec((1, width), lambda b, i: (0, 0))

    bvec = pl.BlockSpec((1, 1, d), lambda b, i: (b, 0, 0))
    blocks = (_nbytes((tm, aw), BF16) + _nbytes((tm, cw), BF16) + 2 * _nbytes((tm, d), F32)
              + _nbytes((tm, d), BF16) + _nbytes((d, d), BF16))
    return pl.pallas_call(
        functools.partial(_out_kernel, alpha=alpha),
        grid=(bsz, s // tm),
        in_specs=[rows(aw), rows(cw), rows(d), pl.BlockSpec((None, d, d), lambda b, i: (layer, 0, 0)), vec(aw), bvec,
                  vec(d), vec(d), bvec, bvec],
        out_specs=[rows(d), rows(d)],
        out_shape=[jax.ShapeDtypeStruct((bsz, s, d), F32), jax.ShapeDtypeStruct((bsz, s, d), BF16)],
        compiler_params=_params(("parallel", "parallel"), _vmem_limit(blocks, temps=4 * _nbytes((tm, d), F32))),
        name="out_proj",
    )(attn, uc, x, w_out, attn_g.reshape(1, aw), gate, ln_g.reshape(1, d), ln_b.reshape(1, d), sc, sh)


def _mlp_kernel(h_ref, w1_ref, b1_ref, w2_ref, b2_ref, x_ref, gate_ref, lg_ref, lb_ref, *rest, alpha, modulate):
    if modulate:
        sc_ref, sh_ref, x_out, h_out, acc_ref = rest
    else:
        x_out, acc_ref = rest
    t, f = pl.program_id(0), pl.program_id(1)
    n_tiles = pl.num_programs(0) - 1
    slot = t % 2

    def chunk():
        hid = jnp.maximum(_dot(h_ref[0], w1_ref[...]) + b1_ref[...], 0.0)
        return _dot((hid * hid).astype(BF16), w2_ref[...])

    def epilogue():
        ff = acc_ref[1 - slot] + b2_ref[...]
        x = _layer_norm(alpha * x_ref[0] + (1.0 + gate_ref[0]) * ff, lg_ref[...], lb_ref[...])
        x_out[0] = x
        if modulate:
            h_out[0] = (x * (1.0 + sc_ref[0]) + sh_ref[0]).astype(BF16)

    @pl.when((t == 0) & (f == 0))
    def _():
        acc_ref[1] = jnp.zeros(acc_ref.shape[1:], F32)

    @pl.when((f == 0) & (t < n_tiles))
    def _():
        acc_ref[slot] = chunk()
        epilogue()

    @pl.when((f > 0) & (t < n_tiles))
    def _():
        acc_ref[slot] += chunk()

    @pl.when((f == 0) & (t == n_tiles))
    def _():
        epilogue()


def _mlp(h, w1, b1, w2, b2, layer, x, gate, ln_g, ln_b, alpha, next_mod=None):
    bsz, s, d = x.shape
    ff_dim = w1.shape[-1]
    tm, tf = _tile(s, 512), _tile(ff_dim, 1024)
    per_batch = s // tm
    n_tiles, n_chunks = bsz * per_batch, ff_dim // tf
    modulate = next_mod is not None

    def prev(t):
        return jnp.maximum(t - 1, 0)

    cur_rows = pl.BlockSpec((1, tm, d), lambda t, f: (jnp.minimum(t, n_tiles - 1), 0, 0))
    prev_rows = pl.BlockSpec((1, tm, d), lambda t, f: (prev(t), 0, 0))
    prev_bvec = pl.BlockSpec((1, 1, d), lambda t, f: (prev(t) // per_batch, 0, 0))
    vec_d = pl.BlockSpec((1, d), lambda t, f: (0, 0))

    def chunk_of(t, f):
        return jnp.where(t == n_tiles, n_chunks - 1, f)

    in_specs = [
        cur_rows,
        pl.BlockSpec((None, d, tf), lambda t, f: (layer, 0, chunk_of(t, f))),
        pl.BlockSpec((1, tf), lambda t, f: (0, chunk_of(t, f))),
        pl.BlockSpec((None, tf, d), lambda t, f: (layer, chunk_of(t, f), 0)),
        vec_d, prev_rows, prev_bvec, vec_d, vec_d,
    ]
    tiles = (n_tiles, tm, d)
    args = [h.reshape(tiles), w1, b1.reshape(1, ff_dim), w2, b2.reshape(1, d), x.reshape(tiles), gate,
            ln_g.reshape(1, d), ln_b.reshape(1, d)]
    out_specs = [prev_rows]
    out_shape = [jax.ShapeDtypeStruct(tiles, F32)]
    blocks = (_nbytes((tm, d), BF16) + 2 * _nbytes((d, tf), BF16) + 2 * _nbytes((tm, d), F32))
    if modulate:
        in_specs += [prev_bvec, prev_bvec]
        args += list(next_mod)
        out_specs.append(prev_rows)
        out_shape.append(jax.ShapeDtypeStruct(tiles, BF16))
        blocks += _nbytes((tm, d), BF16)
    acc = (2, tm, d)
    out = pl.pallas_call(
        functools.partial(_mlp_kernel, alpha=alpha, modulate=modulate),
        grid=(n_tiles + 1, n_chunks),
        in_specs=in_specs,
        out_specs=out_specs,
        out_shape=out_shape,
        scratch_shapes=[pltpu.VMEM(acc, F32)],
        compiler_params=_params(("arbitrary", "arbitrary"),
                                _vmem_limit(blocks, scratch=_nbytes(acc, F32), temps=3 * _nbytes((tm, tf), F32))),
        name="mlp",
    )(*args)
    out = [o.reshape(bsz, s, -1) for o in out]
    return out if modulate else (out[0], None)


def _trunk(x, mods, emb_ln_g, emb_ln_b, w_in, conv_w, conv_b, conv_ln_g, conv_ln_b, attn_out_g, conv_out_g,
           w_out, ln1_g, ln1_b, w_mlp1, b_mlp1, w_mlp2, b_mlp2, ln2_g, ln2_b, bias):
    bsz, s, d = x.shape
    depth = w_in.shape[0]
    aw = d // 2
    cw = d - aw
    alpha = (2 * depth) ** 0.25

    def mod(layer, k):
        return mods[layer, :, k * d:(k + 1) * d].reshape(bsz, 1, d)

    x, h = _embed(x, emb_ln_g, emb_ln_b, mod(0, 1), mod(0, 0))
    for layer in range(depth):
        h2d = h.reshape(bsz * s, d)
        qkv = _qkv_proj(h2d, w_in, layer, 3 * aw).reshape(bsz, s, 3 * aw)
        u = _glu_proj(h2d, w_in, layer, 3 * aw, 3 * aw + cw, cw).reshape(bsz, s, cw)
        attn = _attention(qkv, bias, aw)
        uc = _conv_branch(u, conv_w[layer], conv_b[layer], conv_ln_g[layer], conv_ln_b[layer], conv_out_g[layer])
        x, h = _out_proj(attn, uc, x, w_out, layer, attn_out_g[layer], mod(layer, 2), ln1_g[layer],
                         ln1_b[layer], mod(layer, 4), mod(layer, 3), alpha)
        next_mod = (mod(layer + 1, 1), mod(layer + 1, 0)) if layer + 1 < depth else None
        x, h = _mlp(h, w_mlp1, b_mlp1[layer], w_mlp2, b_mlp2[layer], layer, x, mod(layer, 5),
                    ln2_g[layer], ln2_b[layer], alpha, next_mod)
    return x


def kernel(x_prompt, x_sample, c_prompt, c_sample, emb_ln_g, emb_ln_b, w_ada, b_ada, w_in, conv_w, conv_b,
           conv_ln_g, conv_ln_b, attn_out_g, conv_out_g, w_out, ln1_g, ln1_b, w_mlp1, b_mlp1, w_mlp2, b_mlp2,
           ln2_g, ln2_b):
    d = x_prompt.shape[-1]
    n_prompt, n_sample = c_prompt.shape[0], c_sample.shape[0]
    pad = (-(n_prompt + n_sample)) % 8
    c_all = jnp.concatenate([c_prompt, c_sample, jnp.zeros((pad, d), F32)], axis=0)
    mods = _ada_mod(c_all, w_ada, b_ada)
    n_heads = (d // 2) // HEAD_DIM
    bias = _bias_tiles(n_heads)
    weights = (emb_ln_g, emb_ln_b, w_in.astype(BF16), conv_w, conv_b, conv_ln_g, conv_ln_b, attn_out_g,
               conv_out_g, w_out.astype(BF16), ln1_g, ln1_b, w_mlp1.astype(BF16), b_mlp1, w_mlp2.astype(BF16),
               b_mlp2, ln2_g, ln2_b, bias)
    y_prompt = _trunk(x_prompt, mods[:, :n_prompt], *weights)
    y_sample = _trunk(x_sample, mods[:, n_prompt:n_prompt + n_sample], *weights)
    return (y_prompt, y_sample)
```

```python
import functools

import jax
import jax.numpy as jnp
from jax import lax
from jax.experimental import pallas as pl
from jax.experimental.pallas import tpu as pltpu

F32 = jnp.float32
BF16 = jnp.bfloat16

HEAD_DIM = 64
DILATED_BRANCHES = ((128, 1), (512, 4), (2048, 16))
HALF_WINDOW = 64
CONV_KERNEL = 31
CONV_PAD = (CONV_KERNEL - 1) // 2
N_MOD = 6
LN_EPS = 1e-5
NEG_INF = -1e30

LANES = 128
SUBLANES = 8
VMEM_BYTES_V7X = 64 * 1024 * 1024
VMEM_CAP = VMEM_BYTES_V7X - 8 * 1024 * 1024

Q_BLOCK = 128
K_WINDOW = Q_BLOCK + 2 * HALF_WINDOW
ATTN_UNROLL = 16
ATTN_GATHER_ROWS = 256


def _nbytes(shape, dtype):
    n = 1
    for s in shape:
        n *= s
    return n * jnp.dtype(dtype).itemsize


def _vmem_limit(pipelined, scratch=0, temps=0):
    return int(min(VMEM_CAP, 2 * pipelined + scratch + temps + (2 << 20)))


def _tile(n, preferred, align=LANES):
    t = min(preferred, n) // align * align
    while n % t:
        t -= align
    return t


def _params(semantics, vmem):
    return pltpu.CompilerParams(dimension_semantics=semantics, vmem_limit_bytes=vmem)


def _layer_norm(y, g, b):
    mu = jnp.mean(y, axis=-1, keepdims=True)
    yc = y - mu
    var = jnp.mean(yc * yc, axis=-1, keepdims=True)
    return yc * lax.rsqrt(var + LN_EPS) * g + b


def _rms_norm(y, g):
    return y * lax.rsqrt(jnp.mean(y * y, axis=-1, keepdims=True) + LN_EPS) * g


def _dot(a, b):
    return jnp.dot(a, b, preferred_element_type=F32)


def _ada_kernel(c_ref, w_ref, b_ref, o_ref):
    c = c_ref[...]
    a = c * jax.nn.sigmoid(c)
    w = w_ref[0]
    a_hi = a.astype(BF16)
    a_lo = (a - a_hi.astype(F32)).astype(BF16)
    w_hi = w.astype(BF16)
    w_lo = (w - w_hi.astype(F32)).astype(BF16)
    o_ref[0] = _dot(a_hi, w_hi) + _dot(a_hi, w_lo) + _dot(a_lo, w_hi) + b_ref[0]


def _ada_mod(c, w_ada, b_ada):
    depth, d, n = w_ada.shape
    rows = c.shape[0]
    tn = _tile(n, 1024)
    blocks = _nbytes((rows, d), F32) + _nbytes((d, tn), F32) + 2 * _nbytes((8, tn), F32)
    return pl.pallas_call(
        _ada_kernel,
        grid=(depth, n // tn),
        in_specs=[
            pl.BlockSpec((rows, d), lambda l, j: (0, 0)),
            pl.BlockSpec((1, d, tn), lambda l, j: (l, 0, j)),
            pl.BlockSpec((1, 1, tn), lambda l, j: (l, 0, j)),
        ],
        out_specs=pl.BlockSpec((1, rows, tn), lambda l, j: (l, 0, j)),
        out_shape=jax.ShapeDtypeStruct((depth, rows, n), F32),
        compiler_params=_params(("parallel", "parallel"), _vmem_limit(blocks, temps=2 * _nbytes((d, tn), F32))),
        name="ada_mod",
    )(c, w_ada, b_ada.reshape(depth, 1, n))


def _embed_kernel(x_ref, g_ref, b_ref, sc_ref, sh_ref, x_out, h_out):
    x = _layer_norm(x_ref[0], g_ref[...], b_ref[...])
    x_out[0] = x
    h_out[0] = (x * (1.0 + sc_ref[0]) + sh_ref[0]).astype(BF16)


def _embed(x, g, b, sc, sh):
    bsz, s, d = x.shape
    ts = 256
    row = pl.BlockSpec((1, ts, d), lambda i, j: (i, j, 0))
    vec = pl.BlockSpec((1, d), lambda i, j: (0, 0))
    bvec = pl.BlockSpec((1, 1, d), lambda i, j: (i, 0, 0))
    blocks = 2 * _nbytes((ts, d), F32) + _nbytes((ts, d), BF16)
    return pl.pallas_call(
        _embed_kernel,
        grid=(bsz, s // ts),
        in_specs=[row, vec, vec, bvec, bvec],
        out_specs=[row, row],
        out_shape=[jax.ShapeDtypeStruct((bsz, s, d), F32), jax.ShapeDtypeStruct((bsz, s, d), BF16)],
        compiler_params=_params(("parallel", "parallel"), _vmem_limit(blocks, temps=4 * _nbytes((ts, d), F32))),
        name="embed_ln",
    )(x, g.reshape(1, d), b.reshape(1, d), sc, sh)


def _qkv_kernel(h_ref, w_ref, o_ref):
    o_ref[...] = _dot(h_ref[...], w_ref[...]).astype(o_ref.dtype)


def _qkv_proj(h2d, w_in, layer, n_cols):
    m, d = h2d.shape
    tm, tn = _tile(m, 1024), _tile(n_cols, 1024)
    blocks = _nbytes((tm, d), BF16) + _nbytes((d, tn), BF16) + _nbytes((tm, tn), F32)
    return pl.pallas_call(
        _qkv_kernel,
        grid=(n_cols // tn, m // tm),
        in_specs=[
            pl.BlockSpec((tm, d), lambda n, i: (i, 0)),
            pl.BlockSpec((None, d, tn), lambda n, i: (layer, 0, n)),
        ],
        out_specs=pl.BlockSpec((tm, tn), lambda n, i: (i, n)),
        out_shape=jax.ShapeDtypeStruct((m, n_cols), F32),
        compiler_params=_params(("parallel", "parallel"), _vmem_limit(blocks, temps=2 * _nbytes((tm, tn), F32))),
        name="qkv_proj",
    )(h2d, w_in)


def _qkv_conv_kernel(h_ref, w_ref, prev_ref, cur_ref, next_ref, taps_ref, o_ref, y_ref, sh_ref, *, tiles_per_seq):
    o_ref[...] = _dot(h_ref[...], w_ref[...])

    j = pl.program_id(1) % tiles_per_seq
    rows, lanes = cur_ref.shape[1], cur_ref.shape[2]
    zeros = jnp.zeros((CONV_HALO, lanes), F32)
    sh_ref[0, 0:CONV_HALO, :] = jnp.where(j > 0, prev_ref[0], zeros)
    sh_ref[0, CONV_HALO:CONV_HALO + rows, :] = cur_ref[0]
    sh_ref[0, CONV_HALO + rows:, :] = jnp.where(j < tiles_per_seq - 1, next_ref[0], zeros)
    shifted_rows = rows + 2 * CONV_HALO - SUBLANES
    for b in range(1, SUBLANES):
        sh_ref[b, 0:shifted_rows, :] = sh_ref[0, b:b + shifted_rows, :]

    slab_rows = CONV_ROWS + 2 * CONV_HALO - SUBLANES
    for r0 in range(0, rows, CONV_ROWS):
        acc = jnp.zeros((CONV_ROWS // SUBLANES, SUBLANES, lanes), F32)
        for shift in range(SUBLANES):
            slab = sh_ref[shift, r0:r0 + slab_rows, :].reshape(slab_rows // SUBLANES, SUBLANES, lanes)
            for tap in range(CONV_KERNEL):
                offset = tap + (CONV_HALO - CONV_PAD)
                if offset % SUBLANES == shift:
                    first = offset // SUBLANES
                    acc = acc + slab[first:first + CONV_ROWS // SUBLANES] * taps_ref[tap][None]
        y_ref[0, r0:r0 + CONV_ROWS, :] = acc.reshape(CONV_ROWS, lanes)


def _qkv_conv_proj(h, w_in, layer, n_cols, u, conv_w):
    bsz, s, d = h.shape
    width = u.shape[-1]
    m = bsz * s
    tm = _tile(s, 1024)
    n_groups = width // CONV_LANES
    tn = n_cols // n_groups
    assert tn % LANES == 0 and width % CONV_LANES == 0
    tiles_per_seq = s // tm
    halo_blocks = tm // CONV_HALO
    last_halo = s // CONV_HALO - 1

    def seq_tile(i):
        return i // tiles_per_seq, i % tiles_per_seq

    def prev_map(n, i):
        b, j = seq_tile(i)
        return (b, jnp.maximum(j * halo_blocks - 1, 0), n)

    def cur_map(n, i):
        b, j = seq_tile(i)
        return (b, j, n)

    def next_map(n, i):
        b, j = seq_tile(i)
        return (b, jnp.minimum((j + 1) * halo_blocks, last_halo), n)

    taps = jnp.broadcast_to(conv_w[:, None, :], (CONV_KERNEL, SUBLANES, width))
    shifted = (SUBLANES, tm + 2 * CONV_HALO, CONV_LANES)
    blocks = (_nbytes((tm, d), BF16) + _nbytes((d, tn), BF16) + _nbytes((tm, tn), F32)
              + 2 * _nbytes((CONV_HALO, CONV_LANES), F32) + 2 * _nbytes((tm, CONV_LANES), F32)
              + _nbytes((CONV_KERNEL, SUBLANES, CONV_LANES), F32))
    qkv, y = pl.pallas_call(
        functools.partial(_qkv_conv_kernel, tiles_per_seq=tiles_per_seq),
        grid=(n_groups, m // tm),
        in_specs=[
            pl.BlockSpec((tm, d), lambda n, i: (i, 0)),
            pl.BlockSpec((None, d, tn), lambda n, i: (layer, 0, n)),
            pl.BlockSpec((1, CONV_HALO, CONV_LANES), prev_map),
            pl.BlockSpec((1, tm, CONV_LANES), cur_map),
            pl.BlockSpec((1, CONV_HALO, CONV_LANES), next_map),
            pl.BlockSpec((CONV_KERNEL, SUBLANES, CONV_LANES), lambda n, i: (0, 0, n)),
        ],
        out_specs=[pl.BlockSpec((tm, tn), lambda n, i: (i, n)),
                   pl.BlockSpec((1, tm, CONV_LANES), cur_map)],
        out_shape=[jax.ShapeDtypeStruct((m, n_cols), F32), jax.ShapeDtypeStruct((bsz, s, width), F32)],
        scratch_shapes=[pltpu.VMEM(shifted, F32)],
        compiler_params=_params(("parallel", "parallel"),
                                _vmem_limit(blocks, scratch=_nbytes(shifted, F32), temps=2 * _nbytes((tm, tn), F32))),
        name="qkv_conv_proj",
    )(h.reshape(m, d), w_in, u, u, u, taps)
    return qkv.reshape(bsz, s, n_cols), y


def _conv_norm_kernel(y_ref, cb_ref, lg_ref, lb_ref, og_ref, o_ref):
    y = _layer_norm(y_ref[0] + cb_ref[...], lg_ref[...], lb_ref[...])
    y = y * jax.nn.sigmoid(y)
    o_ref[0] = _rms_norm(y, og_ref[...]).astype(o_ref.dtype)


def _conv_norm(y, conv_b, ln_g, ln_b, out_g):
    bsz, s, width = y.shape
    ts = _tile(s, 512)
    row = pl.BlockSpec((1, ts, width), lambda b, j: (b, j, 0))
    vec = pl.BlockSpec((1, width), lambda b, j: (0, 0))
    blocks = _nbytes((ts, width), F32) + _nbytes((ts, width), BF16)
    return pl.pallas_call(
        _conv_norm_kernel,
        grid=(bsz, s // ts),
        in_specs=[row, vec, vec, vec, vec],
        out_specs=row,
        out_shape=jax.ShapeDtypeStruct((bsz, s, width), BF16),
        compiler_params=_params(("parallel", "parallel"), _vmem_limit(blocks, temps=4 * _nbytes((ts, width), F32))),
        name="conv_norm",
    )(y, conv_b.reshape(1, width), ln_g.reshape(1, width), ln_b.reshape(1, width), out_g.reshape(1, width))


def _glu_kernel(h_ref, wv_ref, wg_ref, o_ref):
    h = h_ref[...]
    o_ref[...] = _dot(h, wv_ref[...]) * jax.nn.sigmoid(_dot(h, wg_ref[...]))


def _glu_proj(h2d, w_in, layer, val_col, gate_col, width):
    m, d = h2d.shape
    tm, tn = _tile(m, 1024), _tile(width, 512)
    v_blk, g_blk = val_col // tn, gate_col // tn
    blocks = _nbytes((tm, d), BF16) + 2 * _nbytes((d, tn), BF16) + _nbytes((tm, tn), F32)
    return pl.pallas_call(
        _glu_kernel,
        grid=(width // tn, m // tm),
        in_specs=[
            pl.BlockSpec((tm, d), lambda n, i: (i, 0)),
            pl.BlockSpec((None, d, tn), lambda n, i: (layer, 0, v_blk + n)),
            pl.BlockSpec((None, d, tn), lambda n, i: (layer, 0, g_blk + n)),
        ],
        out_specs=pl.BlockSpec((tm, tn), lambda n, i: (i, n)),
        out_shape=jax.ShapeDtypeStruct((m, width), F32),
        compiler_params=_params(("parallel", "parallel"), _vmem_limit(blocks, temps=4 * _nbytes((tm, tn), F32))),
        name="glu_proj",
    )(h2d, w_in, w_in)


def _bias_tiles(n_heads):
    slopes = 2.0 ** (-8.0 * jnp.arange(1, n_heads + 1, dtype=F32) / n_heads)
    q = jnp.arange(Q_BLOCK)[:, None]
    c = jnp.arange(K_WINDOW)[None, :]
    branches = []
    for window, dilation in DILATED_BRANCHES:
        assert window == 2 * dilation * HALF_WINDOW
        tiles = []
        for key_start_minus_q_start in (0, -HALF_WINDOW, -2 * HALF_WINDOW):
            rel = key_start_minus_q_start + c - q
            dist = (dilation * jnp.abs(rel)).astype(F32)
            bias = -slopes[:, None, None] * dist[None]
            tiles.append(jnp.where((jnp.abs(rel) <= HALF_WINDOW)[None], bias, NEG_INF))
        branches.append(jnp.stack(tiles, axis=0))
    b = jnp.stack(branches, axis=0)
    b = b.reshape(len(DILATED_BRANCHES), 3, n_heads // 2, 2, Q_BLOCK, K_WINDOW)
    return b.transpose(2, 0, 1, 3, 4, 5)


def _attn_kernel(q_ref, k_ref, v_ref, bias_ref, o_ref, cq_ref, ck_ref, cv_ref, qs_ref, ks_ref, vs_ref,
                 acc_ref, m_ref, l_ref, stage_ref, *, seq):
    lane = lax.broadcasted_iota(jnp.int32, (1, LANES), 1)
    first_head = lane < HEAD_DIM
    scale = HEAD_DIM ** -0.5
    (_, near), (_, mid), (_, far) = DILATED_BRANCHES
    sub = far // mid
    len_mid, len_far = seq // mid, seq // far
    gather_mid_rows = min(ATTN_GATHER_ROWS, len_mid)
    gather_far_rows = min(ATTN_GATHER_ROWS, len_far)

    def window(i, length):
        q0 = pl.multiple_of(i * Q_BLOCK, Q_BLOCK)
        k0 = pl.multiple_of(jnp.clip(q0 - HALF_WINDOW, 0, length - K_WINDOW), HALF_WINDOW)
        return q0, k0, (q0 - k0) // HALF_WINDOW

    def partial_softmax(q, kw, vw, branch, kind):
        tops, dens, nums = [], [], []
        for head in range(2):
            mine = first_head if head == 0 else jnp.logical_not(first_head)
            qh = jnp.where(mine, q, jnp.zeros_like(q))
            s = lax.dot_general(qh, kw, (((1,), (1,)), ((), ())), preferred_element_type=F32)
            s = s + bias_ref[0, branch, kind, head]
            m = jnp.max(s, axis=-1, keepdims=True)
            p = jnp.exp(s - m)
            tops.append(m)
            dens.append(jnp.sum(p, axis=-1, keepdims=True))
            nums.append(_dot(p.astype(BF16), vw))
        return (jnp.where(first_head, tops[0], tops[1]), jnp.where(first_head, dens[0], dens[1]),
                jnp.where(first_head, nums[0], nums[1]))

    def merge(m_old, l_old, o_old, m_blk, l_blk, o_blk):
        m_new = jnp.maximum(m_old, m_blk)
        w_old = jnp.exp(m_old - m_new)
        w_blk = jnp.exp(m_blk - m_new)
        return m_new, w_old * l_old + w_blk * l_blk, w_old * o_old + w_blk * o_blk

    def grouped(n_blocks, block):
        unroll = _tile(n_blocks, ATTN_UNROLL, align=1)

        def group(g, carry):
            for u in range(unroll):
                block(g * unroll + u, u)
            return carry

        lax.fori_loop(0, n_blocks // unroll, group, 0)

    def mid_class(r, carry):
        base = pl.multiple_of(r * len_mid, len_mid)

        def gather_mid(c, carry2):
            c0 = pl.multiple_of(c * gather_mid_rows, gather_mid_rows)
            src = pl.ds(r + mid * c0, gather_mid_rows, stride=mid)
            dst = pl.ds(c0, gather_mid_rows)
            x = q_ref[0, src, :]
            cq_ref[dst, :] = x
            qs_ref[dst, :] = (x * scale).astype(BF16)
            x = k_ref[0, src, :]
            ck_ref[dst, :] = x
            ks_ref[dst, :] = x.astype(BF16)
            x = v_ref[0, src, :]
            cv_ref[dst, :] = x
            vs_ref[dst, :] = x.astype(BF16)
            return carry2

        lax.fori_loop(0, len_mid // gather_mid_rows, gather_mid, 0)

        def mid_block(i, slot):
            q0, k0, kind = window(i, len_mid)
            m, l, o = partial_softmax(qs_ref[pl.ds(q0, Q_BLOCK), :], ks_ref[pl.ds(k0, K_WINDOW), :],
                                      vs_ref[pl.ds(k0, K_WINDOW), :], 1, kind)
            rows = pl.ds(base + q0, Q_BLOCK)
            m_ref[rows, :] = m
            l_ref[rows, :] = l
            acc_ref[rows, :] = o

        grouped(len_mid // Q_BLOCK, mid_block)

        for c in range(sub):
            def gather_far(g, carry2, c=c):
                g0 = pl.multiple_of(g * gather_far_rows, gather_far_rows)
                src = pl.ds(c + sub * g0, gather_far_rows, stride=sub)
                dst = pl.ds(c * len_far + g0, gather_far_rows)
                qs_ref[dst, :] = (cq_ref[src, :] * scale).astype(BF16)
                ks_ref[dst, :] = ck_ref[src, :].astype(BF16)
                vs_ref[dst, :] = cv_ref[src, :].astype(BF16)
                return carry2

            lax.fori_loop(0, len_far // gather_far_rows, gather_far, 0)

        far_blocks = len_far // Q_BLOCK

        def far_block(j, slot):
            c = j // far_blocks
            q0, k0, kind = window(j - c * far_blocks, len_far)
            off = pl.multiple_of(c * len_far, len_far)
            m, l, o = partial_softmax(qs_ref[pl.ds(off + q0, Q_BLOCK), :], ks_ref[pl.ds(off + k0, K_WINDOW), :],
                                      vs_ref[pl.ds(off + k0, K_WINDOW), :], 2, kind)
            rows = pl.ds(base + c + sub * q0, Q_BLOCK, stride=sub)
            m, l, o = merge(m_ref[rows, :], l_ref[rows, :], acc_ref[rows, :], m, l, o)
            m_ref[rows, :] = m
            l_ref[rows, :] = l
            acc_ref[rows, :] = o

        grouped(sub * far_blocks, far_block)
        return carry

    lax.fori_loop(0, mid, mid_class, 0)

    per_class = Q_BLOCK // mid

    def near_block(i, slot):
        q0, k0, kind = window(i, seq)
        m, l, o = partial_softmax((q_ref[0, pl.ds(q0, Q_BLOCK), :] * scale).astype(BF16),
                                  k_ref[0, pl.ds(k0, K_WINDOW), :].astype(BF16),
                                  v_ref[0, pl.ds(k0, K_WINDOW), :].astype(BF16), 0, kind)
        j0 = pl.multiple_of(i * per_class, per_class)
        for c in range(mid):
            src = pl.ds(c * len_mid + j0, per_class)
            dst = pl.ds(c, per_class, stride=mid)
            stage_ref[slot, 0, dst, :] = m_ref[src, :]
            stage_ref[slot, 1, dst, :] = l_ref[src, :]
            stage_ref[slot, 2, dst, :] = acc_ref[src, :]
        m, l, o = merge(stage_ref[slot, 0], stage_ref[slot, 1], stage_ref[slot, 2], m, l, o)
        o_ref[0, pl.ds(q0, Q_BLOCK), :] = (o / l).astype(o_ref.dtype)

    grouped(seq // Q_BLOCK, near_block)


def _attention(qkv, bias, attn_width):
    bsz, seq, _ = qkv.shape
    n_pairs = attn_width // LANES
    (_, near), (_, mid), (_, far) = DILATED_BRANCHES
    assert near == 1 and far % mid == 0 and Q_BLOCK % mid == 0
    assert seq % (far * Q_BLOCK) == 0 and seq // far >= K_WINDOW

    def slab(col0):
        return pl.BlockSpec((1, seq, LANES), lambda p, b: (b, 0, col0 + p))

    bias_block = (1,) + bias.shape[1:]
    mid_rows = (seq // mid, LANES)
    stage = (ATTN_UNROLL, 3, Q_BLOCK, LANES)
    blocks = 3 * _nbytes((seq, LANES), F32) + _nbytes(bias_block, F32) + _nbytes((seq, LANES), BF16)
    scratch = (3 * _nbytes(mid_rows, F32) + 3 * _nbytes(mid_rows, BF16) + 3 * _nbytes((seq, LANES), F32)
               + _nbytes(stage, F32))
    return pl.pallas_call(
        functools.partial(_attn_kernel, seq=seq),
        grid=(n_pairs, bsz),
        in_specs=[slab(0), slab(n_pairs), slab(2 * n_pairs),
                  pl.BlockSpec(bias_block, lambda p, b: (p, 0, 0, 0, 0, 0))],
        out_specs=pl.BlockSpec((1, seq, LANES), lambda p, b: (b, 0, p)),
        out_shape=jax.ShapeDtypeStruct((bsz, seq, attn_width), BF16),
        scratch_shapes=([pltpu.VMEM(mid_rows, F32)] * 3 + [pltpu.VMEM(mid_rows, BF16)] * 3
                        + [pltpu.VMEM((seq, LANES), F32)] * 3 + [pltpu.VMEM(stage, F32)]),
        compiler_params=_params(("parallel", "parallel"),
                                _vmem_limit(blocks, scratch=scratch, temps=16 * _nbytes((Q_BLOCK, K_WINDOW), F32))),
        name="attention",
    )(qkv, qkv, qkv, bias)


CONV_HALO = 16
CONV_ROWS = 32
CONV_LANES = 256
CONV_NORM_ROWS = 64


def _conv_kernel(prev_ref, cur_ref, next_ref, w_ref, cb_ref, lg_ref, lb_ref, og_ref, o_ref, sh_ref, y_ref,
                 *, rows):
    j = pl.program_id(1)
    width = cur_ref.shape[-1]
    zeros = jnp.zeros((CONV_HALO, width), F32)
    sh_ref[0, 0:CONV_HALO, :] = jnp.where(j > 0, prev_ref[0], zeros)
    sh_ref[0, CONV_HALO:CONV_HALO + rows, :] = cur_ref[0]
    sh_ref[0, CONV_HALO + rows:, :] = jnp.where(j < pl.num_programs(1) - 1, next_ref[0], zeros)
    shifted_rows = rows + 2 * CONV_HALO - SUBLANES
    for b in range(1, SUBLANES):
        sh_ref[b, 0:shifted_rows, :] = sh_ref[0, b:b + shifted_rows, :]

    slab_rows = CONV_ROWS + 2 * CONV_HALO - SUBLANES

    def row_block(rb, carry):
        r0 = pl.multiple_of(rb * CONV_ROWS, CONV_ROWS)
        for c0 in range(0, width, CONV_LANES):
            acc = jnp.zeros((CONV_ROWS // SUBLANES, SUBLANES, CONV_LANES), F32)
            for shift in range(SUBLANES):
                slab = sh_ref[shift, pl.ds(r0, slab_rows), c0:c0 + CONV_LANES]
                slab = slab.reshape(slab_rows // SUBLANES, SUBLANES, CONV_LANES)
                for tap in range(CONV_KERNEL):
                    offset = tap + (CONV_HALO - CONV_PAD)
                    if offset % SUBLANES != shift:
                        continue
                    first = offset // SUBLANES
                    window = slab[first:first + CONV_ROWS // SUBLANES]
                    acc = acc + window * w_ref[tap, :, c0:c0 + CONV_LANES][None]
            y_ref[pl.ds(r0, CONV_ROWS), c0:c0 + CONV_LANES] = acc.reshape(CONV_ROWS, CONV_LANES)
        return carry

    lax.fori_loop(0, rows // CONV_ROWS, row_block, 0)

    for r0 in range(0, rows, CONV_NORM_ROWS):
        y = y_ref[r0:r0 + CONV_NORM_ROWS, :] + cb_ref[...]
        y = _layer_norm(y, lg_ref[...], lb_ref[...])
        y = y * jax.nn.sigmoid(y)
        o_ref[0, r0:r0 + CONV_NORM_ROWS, :] = _rms_norm(y, og_ref[...]).astype(o_ref.dtype)


def _conv_branch(u, conv_w, conv_b, ln_g, ln_b, out_g):
    bsz, s, width = u.shape
    rows = 256
    halo_blocks = rows // CONV_HALO
    last_halo = s // CONV_HALO - 1
    vec = pl.BlockSpec((1, width), lambda b, j: (0, 0))
    blocks = (2 * _nbytes((CONV_HALO, width), F32) + _nbytes((rows, width), F32)
              + _nbytes((CONV_KERNEL, SUBLANES, width), F32) + _nbytes((rows, width), BF16))
    shifted = (SUBLANES, rows + 2 * CONV_HALO, width)
    scratch = _nbytes(shifted, F32) + _nbytes((rows, width), F32)
    w_tiles = jnp.broadcast_to(conv_w[:, None, :], (CONV_KERNEL, SUBLANES, width))
    return pl.pallas_call(
        functools.partial(_conv_kernel, rows=rows),
        grid=(bsz, s // rows),
        in_specs=[
            pl.BlockSpec((1, CONV_HALO, width), lambda b, j: (b, jnp.maximum(j * halo_blocks - 1, 0), 0)),
            pl.BlockSpec((1, rows, width), lambda b, j: (b, j, 0)),
            pl.BlockSpec((1, CONV_HALO, width), lambda b, j: (b, jnp.minimum((j + 1) * halo_blocks, last_halo), 0)),
            pl.BlockSpec((CONV_KERNEL, SUBLANES, width), lambda b, j: (0, 0, 0)),
            vec, vec, vec, vec,
        ],
        out_specs=pl.BlockSpec((1, rows, width), lambda b, j: (b, j, 0)),
        out_shape=jax.ShapeDtypeStruct((bsz, s, width), BF16),
        scratch_shapes=[pltpu.VMEM(shifted, F32), pltpu.VMEM((rows, width), F32)],
        compiler_params=_params(("parallel", "parallel"),
                                _vmem_limit(blocks, scratch=scratch, temps=6 * _nbytes((rows, width), F32))),
        name="conv_branch",
    )(u, u, u, w_tiles, conv_b.reshape(1, width), ln_g.reshape(1, width), ln_b.reshape(1, width),
      out_g.reshape(1, width))


OUT_SUBTILES = 2


def _out_kernel(at_ref, uc_ref, x_ref, w_ref, ag_ref, gate_ref, lg_ref, lb_ref, sc_ref, sh_ref, x_out, h_out,
                *, alpha):
    sub = at_ref.shape[1] // OUT_SUBTILES
    for r0 in range(0, at_ref.shape[1], sub):
        rows = pl.ds(r0, sub)
        attn = _rms_norm(at_ref[0, rows, :].astype(F32), ag_ref[...]).astype(BF16)
        mix = _dot(jnp.concatenate([attn, uc_ref[0, rows, :]], axis=-1), w_ref[...])
        x = _layer_norm(alpha * x_ref[0, rows, :] + (1.0 + gate_ref[0]) * mix, lg_ref[...], lb_ref[...])
        x_out[0, rows, :] = x
        h_out[0, rows, :] = (x * (1.0 + sc_ref[0]) + sh_ref[0]).astype(BF16)


def _out_proj(attn, uc, x, w_out, layer, attn_g, gate, ln_g, ln_b, sc, sh, alpha):
    bsz, s, d = x.shape
    aw = attn.shape[-1]
    cw = uc.shape[-1]
    tm = _tile(s, 512)

    def rows(width):
        return pl.BlockSpec((1, tm, width), lambda b, i: (b, i, 0))

    def vec(width):
        return pl.BlockSpec((1, width), lambda b, i: (0, 0))

    bvec = pl.BlockSpec((1, 1, d), lambda b, i: (b, 0, 0))
    blocks = (_nbytes((tm, aw), BF16) + _nbytes((tm, cw), BF16) + 2 * _nbytes((tm, d), F32)
              + _nbytes((tm, d), BF16) + _nbytes((d, d), BF16))
    return pl.pallas_call(
        functools.partial(_out_kernel, alpha=alpha),
        grid=(bsz, s // tm),
        in_specs=[rows(aw), rows(cw), rows(d), pl.BlockSpec((None, d, d), lambda b, i: (layer, 0, 0)), vec(aw), bvec,
                  vec(d), vec(d), bvec, bvec],
        out_specs=[rows(d), rows(d)],
        out_shape=[jax.ShapeDtypeStruct((bsz, s, d), F32), jax.ShapeDtypeStruct((bsz, s, d), BF16)],
        compiler_params=_params(("parallel", "parallel"), _vmem_limit(blocks, temps=4 * _nbytes((tm, d), F32))),
        name="out_proj",
    )(attn, uc, x, w_out, attn_g.reshape(1, aw), gate, ln_g.reshape(1, d), ln_b.reshape(1, d), sc, sh)


def _mlp_kernel(h_ref, w1_ref, b1_ref, w2_ref, b2_ref, x_ref, gate_ref, lg_ref, lb_ref, *rest, alpha, modulate):
    if modulate:
        sc_ref, sh_ref, x_out, h_out, acc_ref = rest
    else:
        x_out, acc_ref = rest
    t, f = pl.program_id(0), pl.program_id(1)
    n_tiles = pl.num_programs(0) - 1
    slot = t % 2

    def chunk():
        hid = jnp.maximum(_dot(h_ref[0], w1_ref[...]) + b1_ref[...], 0.0)
        return _dot((hid * hid).astype(BF16), w2_ref[...])

    def epilogue(src):
        ff = acc_ref[src] + b2_ref[...]
        x = _layer_norm(alpha * x_ref[0] + (1.0 + gate_ref[0]) * ff, lg_ref[...], lb_ref[...])
        x_out[0] = x
        if modulate:
            h_out[0] = (x * (1.0 + sc_ref[0]) + sh_ref[0]).astype(BF16)

    @pl.when((t == 0) & (f == 0))
    def _():
        acc_ref[1] = jnp.zeros(acc_ref.shape[1:], F32)

    for cur in range(2):
        mine = slot == cur

        @pl.when(mine & (f == 0) & (t < n_tiles))
        def _(cur=cur):
            acc_ref[cur] = chunk()
            epilogue(1 - cur)

        @pl.when(mine & (f > 0) & (t < n_tiles))
        def _(cur=cur):
            acc_ref[cur] += chunk()

        @pl.when(mine & (f == 0) & (t == n_tiles))
        def _(cur=cur):
            epilogue(1 - cur)


def _mlp(h, w1, b1, w2, b2, layer, x, gate, ln_g, ln_b, alpha, next_mod=None):
    bsz, s, d = x.shape
    ff_dim = w1.shape[-1]
    tm, tf = _tile(s, 512), _tile(ff_dim, 1024)
    per_batch = s // tm
    n_tiles, n_chunks = bsz * per_batch, ff_dim // tf
    modulate = next_mod is not None

    def prev(t):
        return jnp.maximum(t - 1, 0)

    cur_rows = pl.BlockSpec((1, tm, d), lambda t, f: (jnp.minimum(t, n_tiles - 1), 0, 0))
    prev_rows = pl.BlockSpec((1, tm, d), lambda t, f: (prev(t), 0, 0))
    prev_bvec = pl.BlockSpec((1, 1, d), lambda t, f: (prev(t) // per_batch, 0, 0))
    vec_d = pl.BlockSpec((1, d), lambda t, f: (0, 0))

    def chunk_of(t, f):
        return jnp.where(t == n_tiles, n_chunks - 1, f)

    in_specs = [
        cur_rows,
        pl.BlockSpec((None, d, tf), lambda t, f: (layer, 0, chunk_of(t, f))),
        pl.BlockSpec((1, tf), lambda t, f: (0, chunk_of(t, f))),
        pl.BlockSpec((None, tf, d), lambda t, f: (layer, chunk_of(t, f), 0)),
        vec_d, prev_rows, prev_bvec, vec_d, vec_d,
    ]
    tiles = (n_tiles, tm, d)
    args = [h.reshape(tiles), w1, b1.reshape(1, ff_dim), w2, b2.reshape(1, d), x.reshape(tiles), gate,
            ln_g.reshape(1, d), ln_b.reshape(1, d)]
    out_specs = [prev_rows]
    out_shape = [jax.ShapeDtypeStruct(tiles, F32)]
    blocks = (_nbytes((tm, d), BF16) + 2 * _nbytes((d, tf), BF16) + 2 * _nbytes((tm, d), F32))
    if modulate:
        in_specs += [prev_bvec, prev_bvec]
        args += list(next_mod)
        out_specs.append(prev_rows)
        out_shape.append(jax.ShapeDtypeStruct(tiles, BF16))
        blocks += _nbytes((tm, d), BF16)
    acc = (2, tm, d)
    out = pl.pallas_call(
        functools.partial(_mlp_kernel, alpha=alpha, modulate=modulate),
        grid=(n_tiles + 1, n_chunks),
        in_specs=in_specs,
        out_specs=out_specs,
        out_shape=out_shape,
        scratch_shapes=[pltpu.VMEM(acc, F32)],
        compiler_params=_params(("arbitrary", "arbitrary"),
                                _vmem_limit(blocks, scratch=_nbytes(acc, F32), temps=3 * _nbytes((tm, tf), F32))),
        name="mlp",
    )(*args)
    out = [o.reshape(bsz, s, -1) for o in out]
    return out if modulate else (out[0], None)


def _trunk(x, mods, emb_ln_g, emb_ln_b, w_in, conv_w, conv_b, conv_ln_g, conv_ln_b, attn_out_g, conv_out_g,
           w_out, ln1_g, ln1_b, w_mlp1, b_mlp1, w_mlp2, b_mlp2, ln2_g, ln2_b, bias):
    bsz, s, d = x.shape
    depth = w_in.shape[0]
    aw = d // 2
    cw = d - aw
    alpha = (2 * depth) ** 0.25

    def mod(layer, k):
        return mods[layer, :, k * d:(k + 1) * d].reshape(bsz, 1, d)

    x, h = _embed(x, emb_ln_g, emb_ln_b, mod(0, 1), mod(0, 0))
    for layer in range(depth):
        h2d = h.reshape(bsz * s, d)
        u = _glu_proj(h2d, w_in, layer, 3 * aw, 3 * aw + cw, cw).reshape(bsz, s, cw)
        qkv, y = _qkv_conv_proj(h, w_in, layer, 3 * aw, u, conv_w[layer])
        attn = _attention(qkv, bias, aw)
        uc = _conv_norm(y, conv_b[layer], conv_ln_g[layer], conv_ln_b[layer], conv_out_g[layer])
        x, h = _out_proj(attn, uc, x, w_out, layer, attn_out_g[layer], mod(layer, 2), ln1_g[layer],
                         ln1_b[layer], mod(layer, 4), mod(layer, 3), alpha)
        next_mod = (mod(layer + 1, 1), mod(layer + 1, 0)) if layer + 1 < depth else None
        x, h = _mlp(h, w_mlp1, b_mlp1[layer], w_mlp2, b_mlp2[layer], layer, x, mod(layer, 5),
                    ln2_g[layer], ln2_b[layer], alpha, next_mod)
    return x


def kernel(x_prompt, x_sample, c_prompt, c_sample, emb_ln_g, emb_ln_b, w_ada, b_ada, w_in, conv_w, conv_b,
           conv_ln_g, conv_ln_b, attn_out_g, conv_out_g, w_out, ln1_g, ln1_b, w_mlp1, b_mlp1, w_mlp2, b_mlp2,
           ln2_g, ln2_b):
    d = x_prompt.shape[-1]
    n_prompt, n_sample = c_prompt.shape[0], c_sample.shape[0]
    pad = (-(n_prompt + n_sample)) % 8
    c_all = jnp.concatenate([c_prompt, c_sample, jnp.zeros((pad, d), F32)], axis=0)
    mods = _ada_mod(c_all, w_ada, b_ada)
    n_heads = (d // 2) // HEAD_DIM
    bias = _bias_tiles(n_heads)
    weights = (emb_ln_g, emb_ln_b, w_in.astype(BF16), conv_w, conv_b, conv_ln_g, conv_ln_b, attn_out_g,
               conv_out_g, w_out.astype(BF16), ln1_g, ln1_b, w_mlp1.astype(BF16), b_mlp1, w_mlp2.astype(BF16),
               b_mlp2, ln2_g, ln2_b, bias)
    y_prompt = _trunk(x_prompt, mods[:, :n_prompt], *weights)
    y_sample = _trunk(x_sample, mods[:, n_prompt:n_prompt + n_sample], *weights)
    return (y_prompt, y_sample)
```

```python
import functools

import jax
import jax.numpy as jnp
from jax import lax
from jax.experimental import pallas as pl
from jax.experimental.pallas import tpu as pltpu

F32 = jnp.float32
BF16 = jnp.bfloat16

HEAD_DIM = 64
DILATED_BRANCHES = ((128, 1), (512, 4), (2048, 16))
HALF_WINDOW = 64
CONV_KERNEL = 31
CONV_PAD = (CONV_KERNEL - 1) // 2
N_MOD = 6
LN_EPS = 1e-5
NEG_INF = -1e30

LANES = 128
SUBLANES = 8
VMEM_BYTES_V7X = 64 * 1024 * 1024
VMEM_CAP = VMEM_BYTES_V7X - 8 * 1024 * 1024

Q_BLOCK = 128
K_WINDOW = Q_BLOCK + 2 * HALF_WINDOW
ATTN_UNROLL = 16
ATTN_GATHER_ROWS = 256


def _nbytes(shape, dtype):
    n = 1
    for s in shape:
        n *= s
    return n * jnp.dtype(dtype).itemsize


def _vmem_limit(pipelined, scratch=0, temps=0):
    return int(min(VMEM_CAP, 2 * pipelined + scratch + temps + (2 << 20)))


def _tile(n, preferred, align=LANES):
    t = min(preferred, n) // align * align
    while n % t:
        t -= align
    return t


def _params(semantics, vmem):
    return pltpu.CompilerParams(dimension_semantics=semantics, vmem_limit_bytes=vmem)


def _layer_norm(y, g, b):
    mu = jnp.mean(y, axis=-1, keepdims=True)
    yc = y - mu
    var = jnp.mean(yc * yc, axis=-1, keepdims=True)
    return yc * lax.rsqrt(var + LN_EPS) * g + b


def _rms_norm(y, g):
    return y * lax.rsqrt(jnp.mean(y * y, axis=-1, keepdims=True) + LN_EPS) * g


def _dot(a, b):
    return jnp.dot(a, b, preferred_element_type=F32)


def _ada_kernel(c_ref, w_ref, b_ref, o_ref):
    c = c_ref[...]
    a = c * jax.nn.sigmoid(c)
    w = w_ref[0]
    a_hi = a.astype(BF16)
    a_lo = (a - a_hi.astype(F32)).astype(BF16)
    w_hi = w.astype(BF16)
    w_lo = (w - w_hi.astype(F32)).astype(BF16)
    o_ref[0] = _dot(a_hi, w_hi) + _dot(a_hi, w_lo) + _dot(a_lo, w_hi) + b_ref[0]


def _ada_mod(c, w_ada, b_ada):
    depth, d, n = w_ada.shape
    rows = c.shape[0]
    tn = _tile(n, 1024)
    blocks = _nbytes((rows, d), F32) + _nbytes((d, tn), F32) + 2 * _nbytes((8, tn), F32)
    return pl.pallas_call(
        _ada_kernel,
        grid=(depth, n // tn),
        in_specs=[
            pl.BlockSpec((rows, d), lambda l, j: (0, 0)),
            pl.BlockSpec((1, d, tn), lambda l, j: (l, 0, j)),
            pl.BlockSpec((1, 1, tn), lambda l, j: (l, 0, j)),
        ],
        out_specs=pl.BlockSpec((1, rows, tn), lambda l, j: (l, 0, j)),
        out_shape=jax.ShapeDtypeStruct((depth, rows, n), F32),
        compiler_params=_params(("parallel", "parallel"), _vmem_limit(blocks, temps=2 * _nbytes((d, tn), F32))),
        name="ada_mod",
    )(c, w_ada, b_ada.reshape(depth, 1, n))


def _embed_kernel(x_ref, g_ref, b_ref, sc_ref, sh_ref, x_out, h_out):
    x = _layer_norm(x_ref[0], g_ref[...], b_ref[...])
    x_out[0] = x
    h_out[0] = (x * (1.0 + sc_ref[0]) + sh_ref[0]).astype(BF16)


def _embed(x, g, b, sc, sh):
    bsz, s, d = x.shape
    ts = 256
    row = pl.BlockSpec((1, ts, d), lambda i, j: (i, j, 0))
    vec = pl.BlockSpec((1, d), lambda i, j: (0, 0))
    bvec = pl.BlockSpec((1, 1, d), lambda i, j: (i, 0, 0))
    blocks = 2 * _nbytes((ts, d), F32) + _nbytes((ts, d), BF16)
    return pl.pallas_call(
        _embed_kernel,
        grid=(bsz, s // ts),
        in_specs=[row, vec, vec, bvec, bvec],
        out_specs=[row, row],
        out_shape=[jax.ShapeDtypeStruct((bsz, s, d), F32), jax.ShapeDtypeStruct((bsz, s, d), BF16)],
        compiler_params=_params(("parallel", "parallel"), _vmem_limit(blocks, temps=4 * _nbytes((ts, d), F32))),
        name="embed_ln",
    )(x, g.reshape(1, d), b.reshape(1, d), sc, sh)


def _qkv_kernel(h_ref, w_ref, o_ref):
    o_ref[...] = _dot(h_ref[...], w_ref[...]).astype(o_ref.dtype)


def _qkv_proj(h2d, w_in, layer, n_cols):
    m, d = h2d.shape
    tm, tn = _tile(m, 1024), _tile(n_cols, 1024)
    blocks = _nbytes((tm, d), BF16) + _nbytes((d, tn), BF16) + _nbytes((tm, tn), F32)
    return pl.pallas_call(
        _qkv_kernel,
        grid=(n_cols // tn, m // tm),
        in_specs=[
            pl.BlockSpec((tm, d), lambda n, i: (i, 0)),
            pl.BlockSpec((None, d, tn), lambda n, i: (layer, 0, n)),
        ],
        out_specs=pl.BlockSpec((tm, tn), lambda n, i: (i, n)),
        out_shape=jax.ShapeDtypeStruct((m, n_cols), F32),
        compiler_params=_params(("parallel", "parallel"), _vmem_limit(blocks, temps=2 * _nbytes((tm, tn), F32))),
        name="qkv_proj",
    )(h2d, w_in)


def _qkv_conv_kernel(h_ref, w_ref, prev_ref, cur_ref, next_ref, taps_ref, o_ref, y_ref, sh_ref, *, tiles_per_seq):
    o_ref[...] = _dot(h_ref[...], w_ref[...])

    j = pl.program_id(1) % tiles_per_seq
    rows, lanes = cur_ref.shape[1], cur_ref.shape[2]
    zeros = jnp.zeros((CONV_HALO, lanes), F32)
    sh_ref[0, 0:CONV_HALO, :] = jnp.where(j > 0, prev_ref[0], zeros)
    sh_ref[0, CONV_HALO:CONV_HALO + rows, :] = cur_ref[0]
    sh_ref[0, CONV_HALO + rows:, :] = jnp.where(j < tiles_per_seq - 1, next_ref[0], zeros)
    shifted_rows = rows + 2 * CONV_HALO - SUBLANES
    for b in range(1, SUBLANES):
        sh_ref[b, 0:shifted_rows, :] = sh_ref[0, b:b + shifted_rows, :]

    slab_rows = CONV_ROWS + 2 * CONV_HALO - SUBLANES
    for r0 in range(0, rows, CONV_ROWS):
        acc = jnp.zeros((CONV_ROWS // SUBLANES, SUBLANES, lanes), F32)
        for shift in range(SUBLANES):
            slab = sh_ref[shift, r0:r0 + slab_rows, :].reshape(slab_rows // SUBLANES, SUBLANES, lanes)
            for tap in range(CONV_KERNEL):
                offset = tap + (CONV_HALO - CONV_PAD)
                if offset % SUBLANES == shift:
                    first = offset // SUBLANES
                    acc = acc + slab[first:first + CONV_ROWS // SUBLANES] * taps_ref[tap][None]
        y_ref[0, r0:r0 + CONV_ROWS, :] = acc.reshape(CONV_ROWS, lanes)


def _qkv_conv_proj(h, w_in, layer, n_cols, u, conv_w):
    bsz, s, d = h.shape
    width = u.shape[-1]
    m = bsz * s
    tm = _tile(s, 1024)
    n_groups = width // CONV_LANES
    tn = n_cols // n_groups
    assert tn % LANES == 0 and width % CONV_LANES == 0
    tiles_per_seq = s // tm
    halo_blocks = tm // CONV_HALO
    last_halo = s // CONV_HALO - 1

    def seq_tile(i):
        return i // tiles_per_seq, i % tiles_per_seq

    def prev_map(n, i):
        b, j = seq_tile(i)
        return (b, jnp.maximum(j * halo_blocks - 1, 0), n)

    def cur_map(n, i):
        b, j = seq_tile(i)
        return (b, j, n)

    def next_map(n, i):
        b, j = seq_tile(i)
        return (b, jnp.minimum((j + 1) * halo_blocks, last_halo), n)

    taps = jnp.broadcast_to(conv_w[:, None, :], (CONV_KERNEL, SUBLANES, width))
    shifted = (SUBLANES, tm + 2 * CONV_HALO, CONV_LANES)
    blocks = (_nbytes((tm, d), BF16) + _nbytes((d, tn), BF16) + _nbytes((tm, tn), F32)
              + 2 * _nbytes((CONV_HALO, CONV_LANES), F32) + 2 * _nbytes((tm, CONV_LANES), F32)
              + _nbytes((CONV_KERNEL, SUBLANES, CONV_LANES), F32))
    qkv, y = pl.pallas_call(
        functools.partial(_qkv_conv_kernel, tiles_per_seq=tiles_per_seq),
        grid=(n_groups, m // tm),
        in_specs=[
            pl.BlockSpec((tm, d), lambda n, i: (i, 0)),
            pl.BlockSpec((None, d, tn), lambda n, i: (layer, 0, n)),
            pl.BlockSpec((1, CONV_HALO, CONV_LANES), prev_map),
            pl.BlockSpec((1, tm, CONV_LANES), cur_map),
            pl.BlockSpec((1, CONV_HALO, CONV_LANES), next_map),
            pl.BlockSpec((CONV_KERNEL, SUBLANES, CONV_LANES), lambda n, i: (0, 0, n)),
        ],
        out_specs=[pl.BlockSpec((tm, tn), lambda n, i: (i, n)),
                   pl.BlockSpec((1, tm, CONV_LANES), cur_map)],
        out_shape=[jax.ShapeDtypeStruct((m, n_cols), F32), jax.ShapeDtypeStruct((bsz, s, width), F32)],
        scratch_shapes=[pltpu.VMEM(shifted, F32)],
        compiler_params=_params(("parallel", "parallel"),
                                _vmem_limit(blocks, scratch=_nbytes(shifted, F32), temps=2 * _nbytes((tm, tn), F32))),
        name="qkv_conv_proj",
    )(h.reshape(m, d), w_in, u, u, u, taps)
    return qkv.reshape(bsz, s, n_cols), y


def _conv_norm_kernel(y_ref, cb_ref, lg_ref, lb_ref, og_ref, o_ref):
    y = _layer_norm(y_ref[0] + cb_ref[...], lg_ref[...], lb_ref[...])
    y = y * jax.nn.sigmoid(y)
    o_ref[0] = _rms_norm(y, og_ref[...]).astype(o_ref.dtype)


def _conv_norm(y, conv_b, ln_g, ln_b, out_g):
    bsz, s, width = y.shape
    ts = _tile(s, 512)
    row = pl.BlockSpec((1, ts, width), lambda b, j: (b, j, 0))
    vec = pl.BlockSpec((1, width), lambda b, j: (0, 0))
    blocks = _nbytes((ts, width), F32) + _nbytes((ts, width), BF16)
    return pl.pallas_call(
        _conv_norm_kernel,
        grid=(bsz, s // ts),
        in_specs=[row, vec, vec, vec, vec],
        out_specs=row,
        out_shape=jax.ShapeDtypeStruct((bsz, s, width), BF16),
        compiler_params=_params(("parallel", "parallel"), _vmem_limit(blocks, temps=4 * _nbytes((ts, width), F32))),
        name="conv_norm",
    )(y, conv_b.reshape(1, width), ln_g.reshape(1, width), ln_b.reshape(1, width), out_g.reshape(1, width))


def _glu_kernel(h_ref, wv_ref, wg_ref, o_ref):
    h = h_ref[...]
    o_ref[...] = _dot(h, wv_ref[...]) * jax.nn.sigmoid(_dot(h, wg_ref[...]))


def _glu_proj(h2d, w_in, layer, val_col, gate_col, width):
    m, d = h2d.shape
    tm, tn = _tile(m, 1024), _tile(width, 512)
    v_blk, g_blk = val_col // tn, gate_col // tn
    blocks = _nbytes((tm, d), BF16) + 2 * _nbytes((d, tn), BF16) + _nbytes((tm, tn), F32)
    return pl.pallas_call(
        _glu_kernel,
        grid=(width // tn, m // tm),
        in_specs=[
            pl.BlockSpec((tm, d), lambda n, i: (i, 0)),
            pl.BlockSpec((None, d, tn), lambda n, i: (layer, 0, v_blk + n)),
            pl.BlockSpec((None, d, tn), lambda n, i: (layer, 0, g_blk + n)),
        ],
        out_specs=pl.BlockSpec((tm, tn), lambda n, i: (i, n)),
        out_shape=jax.ShapeDtypeStruct((m, width), F32),
        compiler_params=_params(("parallel", "parallel"), _vmem_limit(blocks, temps=4 * _nbytes((tm, tn), F32))),
        name="glu_proj",
    )(h2d, w_in, w_in)


def _bias_tiles(n_heads):
    slopes = 2.0 ** (-8.0 * jnp.arange(1, n_heads + 1, dtype=F32) / n_heads)
    q = jnp.arange(Q_BLOCK)[:, None]
    c = jnp.arange(K_WINDOW)[None, :]
    branches = []
    for window, dilation in DILATED_BRANCHES:
        assert window == 2 * dilation * HALF_WINDOW
        tiles = []
        for key_start_minus_q_start in (0, -HALF_WINDOW, -2 * HALF_WINDOW):
            rel = key_start_minus_q_start + c - q
            dist = (dilation * jnp.abs(rel)).astype(F32)
            bias = -slopes[:, None, None] * dist[None]
            tiles.append(jnp.where((jnp.abs(rel) <= HALF_WINDOW)[None], bias, NEG_INF))
        branches.append(jnp.stack(tiles, axis=0))
    b = jnp.stack(branches, axis=0)
    b = b.reshape(len(DILATED_BRANCHES), 3, n_heads // 2, 2, Q_BLOCK, K_WINDOW)
    return b.transpose(2, 0, 1, 3, 4, 5)


def _attn_kernel(q_ref, k_ref, v_ref, bias_ref, o_ref, cq_ref, ck_ref, cv_ref, qs_ref, ks_ref, vs_ref,
                 acc_ref, m_ref, l_ref, stage_ref, *, seq):
    lane = lax.broadcasted_iota(jnp.int32, (1, LANES), 1)
    first_head = lane < HEAD_DIM
    scale = HEAD_DIM ** -0.5
    (_, near), (_, mid), (_, far) = DILATED_BRANCHES
    sub = far // mid
    len_mid, len_far = seq // mid, seq // far
    gather_mid_rows = min(ATTN_GATHER_ROWS, len_mid)
    gather_far_rows = min(ATTN_GATHER_ROWS, len_far)

    def window(i, length):
        q0 = pl.multiple_of(i * Q_BLOCK, Q_BLOCK)
        k0 = pl.multiple_of(jnp.clip(q0 - HALF_WINDOW, 0, length - K_WINDOW), HALF_WINDOW)
        return q0, k0, (q0 - k0) // HALF_WINDOW

    def partial_softmax(q, kw, vw, branch, kind):
        tops, probs = [], []
        for head in range(2):
            mine = first_head if head == 0 else jnp.logical_not(first_head)
            qh = jnp.where(mine, q, jnp.zeros_like(q))
            s = lax.dot_general(qh, kw, (((1,), (1,)), ((), ())), preferred_element_type=F32)
            s = s + bias_ref[0, branch, kind, head]
            m = jnp.max(s, axis=-1, keepdims=True)
            tops.append(m)
            probs.append(jnp.exp(s - m).astype(BF16))
        zero = jnp.zeros_like(vw)
        ind = jnp.where(first_head, 1.0, 0.0).astype(BF16)
        rhs = jnp.concatenate(
            [jnp.concatenate([jnp.where(first_head, vw, zero), jnp.broadcast_to(ind, vw.shape)], axis=1),
             jnp.concatenate([jnp.where(first_head, zero, vw), jnp.broadcast_to(1 - ind, vw.shape)], axis=1)],
            axis=0)
        res = _dot(jnp.concatenate(probs, axis=1), rhs)
        return jnp.where(first_head, tops[0], tops[1]), res[:, LANES:], res[:, :LANES]

    def merge(m_old, l_old, o_old, m_blk, l_blk, o_blk):
        m_new = jnp.maximum(m_old, m_blk)
        w_old = jnp.exp(m_old - m_new)
        w_blk = jnp.exp(m_blk - m_new)
        return m_new, w_old * l_old + w_blk * l_blk, w_old * o_old + w_blk * o_blk

    def grouped(n_blocks, block):
        unroll = _tile(n_blocks, ATTN_UNROLL, align=1)

        def group(g, carry):
            for u in range(unroll):
                block(g * unroll + u, u)
            return carry

        lax.fori_loop(0, n_blocks // unroll, group, 0)

    def mid_class(r, carry):
        base = pl.multiple_of(r * len_mid, len_mid)

        def gather_mid(c, carry2):
            c0 = pl.multiple_of(c * gather_mid_rows, gather_mid_rows)
            src = pl.ds(r + mid * c0, gather_mid_rows, stride=mid)
            dst = pl.ds(c0, gather_mid_rows)
            x = q_ref[0, src, :]
            cq_ref[dst, :] = x
            qs_ref[dst, :] = (x * scale).astype(BF16)
            x = k_ref[0, src, :]
            ck_ref[dst, :] = x
            ks_ref[dst, :] = x.astype(BF16)
            x = v_ref[0, src, :]
            cv_ref[dst, :] = x
            vs_ref[dst, :] = x.astype(BF16)
            return carry2

        lax.fori_loop(0, len_mid // gather_mid_rows, gather_mid, 0)

        def mid_block(i, slot):
            q0, k0, kind = window(i, len_mid)
            m, l, o = partial_softmax(qs_ref[pl.ds(q0, Q_BLOCK), :], ks_ref[pl.ds(k0, K_WINDOW), :],
                                      vs_ref[pl.ds(k0, K_WINDOW), :], 1, kind)
            rows = pl.ds(base + q0, Q_BLOCK)
            m_ref[rows, :] = m
            l_ref[rows, :] = l
            acc_ref[rows, :] = o

        grouped(len_mid // Q_BLOCK, mid_block)

        for c in range(sub):
            def gather_far(g, carry2, c=c):
                g0 = pl.multiple_of(g * gather_far_rows, gather_far_rows)
                src = pl.ds(c + sub * g0, gather_far_rows, stride=sub)
                dst = pl.ds(c * len_far + g0, gather_far_rows)
                qs_ref[dst, :] = (cq_ref[src, :] * scale).astype(BF16)
                ks_ref[dst, :] = ck_ref[src, :].astype(BF16)
                vs_ref[dst, :] = cv_ref[src, :].astype(BF16)
                return carry2

            lax.fori_loop(0, len_far // gather_far_rows, gather_far, 0)

        far_blocks = len_far // Q_BLOCK

        def far_block(j, slot):
            c = j // far_blocks
            q0, k0, kind = window(j - c * far_blocks, len_far)
            off = pl.multiple_of(c * len_far, len_far)
            m, l, o = partial_softmax(qs_ref[pl.ds(off + q0, Q_BLOCK), :], ks_ref[pl.ds(off + k0, K_WINDOW), :],
                                      vs_ref[pl.ds(off + k0, K_WINDOW), :], 2, kind)
            rows = pl.ds(base + c + sub * q0, Q_BLOCK, stride=sub)
            m, l, o = merge(m_ref[rows, :], l_ref[rows, :], acc_ref[rows, :], m, l, o)
            m_ref[rows, :] = m
            l_ref[rows, :] = l
            acc_ref[rows, :] = o

        grouped(sub * far_blocks, far_block)
        return carry

    lax.fori_loop(0, mid, mid_class, 0)

    per_class = Q_BLOCK // mid

    def near_block(i, slot):
        q0, k0, kind = window(i, seq)
        m, l, o = partial_softmax((q_ref[0, pl.ds(q0, Q_BLOCK), :] * scale).astype(BF16),
                                  k_ref[0, pl.ds(k0, K_WINDOW), :].astype(BF16),
                                  v_ref[0, pl.ds(k0, K_WINDOW), :].astype(BF16), 0, kind)
        j0 = pl.multiple_of(i * per_class, per_class)
        for c in range(mid):
            src = pl.ds(c * len_mid + j0, per_class)
            dst = pl.ds(c, per_class, stride=mid)
            stage_ref[slot, 0, dst, :] = m_ref[src, :]
            stage_ref[slot, 1, dst, :] = l_ref[src, :]
            stage_ref[slot, 2, dst, :] = acc_ref[src, :]
        m, l, o = merge(stage_ref[slot, 0], stage_ref[slot, 1], stage_ref[slot, 2], m, l, o)
        o_ref[0, pl.ds(q0, Q_BLOCK), :] = (o / l).astype(o_ref.dtype)

    grouped(seq // Q_BLOCK, near_block)


def _attention(qkv, bias, attn_width):
    bsz, seq, _ = qkv.shape
    n_pairs = attn_width // LANES
    (_, near), (_, mid), (_, far) = DILATED_BRANCHES
    assert near == 1 and far % mid == 0 and Q_BLOCK % mid == 0
    assert seq % (far * Q_BLOCK) == 0 and seq // far >= K_WINDOW

    def slab(col0):
        return pl.BlockSpec((1, seq, LANES), lambda p, b: (b, 0, col0 + p))

    bias_block = (1,) + bias.shape[1:]
    mid_rows = (seq // mid, LANES)
    stage = (ATTN_UNROLL, 3, Q_BLOCK, LANES)
    blocks = 3 * _nbytes((seq, LANES), F32) + _nbytes(bias_block, F32) + _nbytes((seq, LANES), BF16)
    scratch = (3 * _nbytes(mid_rows, F32) + 3 * _nbytes(mid_rows, BF16) + 3 * _nbytes((seq, LANES), F32)
               + _nbytes(stage, F32))
    return pl.pallas_call(
        functools.partial(_attn_kernel, seq=seq),
        grid=(n_pairs, bsz),
        in_specs=[slab(0), slab(n_pairs), slab(2 * n_pairs),
                  pl.BlockSpec(bias_block, lambda p, b: (p, 0, 0, 0, 0, 0))],
        out_specs=pl.BlockSpec((1, seq, LANES), lambda p, b: (b, 0, p)),
        out_shape=jax.ShapeDtypeStruct((bsz, seq, attn_width), BF16),
        scratch_shapes=([pltpu.VMEM(mid_rows, F32)] * 3 + [pltpu.VMEM(mid_rows, BF16)] * 3
                        + [pltpu.VMEM((seq, LANES), F32)] * 3 + [pltpu.VMEM(stage, F32)]),
        compiler_params=_params(("parallel", "parallel"),
                                _vmem_limit(blocks, scratch=scratch, temps=16 * _nbytes((Q_BLOCK, K_WINDOW), F32))),
        name="attention",
    )(qkv, qkv, qkv, bias)


CONV_HALO = 16
CONV_ROWS = 32
CONV_LANES = 256
CONV_NORM_ROWS = 64


def _conv_kernel(prev_ref, cur_ref, next_ref, w_ref, cb_ref, lg_ref, lb_ref, og_ref, o_ref, sh_ref, y_ref,
                 *, rows):
    j = pl.program_id(1)
    width = cur_ref.shape[-1]
    zeros = jnp.zeros((CONV_HALO, width), F32)
    sh_ref[0, 0:CONV_HALO, :] = jnp.where(j > 0, prev_ref[0], zeros)
    sh_ref[0, CONV_HALO:CONV_HALO + rows, :] = cur_ref[0]
    sh_ref[0, CONV_HALO + rows:, :] = jnp.where(j < pl.num_programs(1) - 1, next_ref[0], zeros)
    shifted_rows = rows + 2 * CONV_HALO - SUBLANES
    for b in range(1, SUBLANES):
        sh_ref[b, 0:shifted_rows, :] = sh_ref[0, b:b + shifted_rows, :]

    slab_rows = CONV_ROWS + 2 * CONV_HALO - SUBLANES

    def row_block(rb, carry):
        r0 = pl.multiple_of(rb * CONV_ROWS, CONV_ROWS)
        for c0 in range(0, width, CONV_LANES):
            acc = jnp.zeros((CONV_ROWS // SUBLANES, SUBLANES, CONV_LANES), F32)
            for shift in range(SUBLANES):
                slab = sh_ref[shift, pl.ds(r0, slab_rows), c0:c0 + CONV_LANES]
                slab = slab.reshape(slab_rows // SUBLANES, SUBLANES, CONV_LANES)
                for tap in range(CONV_KERNEL):
                    offset = tap + (CONV_HALO - CONV_PAD)
                    if offset % SUBLANES != shift:
                        continue
                    first = offset // SUBLANES
                    window = slab[first:first + CONV_ROWS // SUBLANES]
                    acc = acc + window * w_ref[tap, :, c0:c0 + CONV_LANES][None]
            y_ref[pl.ds(r0, CONV_ROWS), c0:c0 + CONV_LANES] = acc.reshape(CONV_ROWS, CONV_LANES)
        return carry

    lax.fori_loop(0, rows // CONV_ROWS, row_block, 0)

    for r0 in range(0, rows, CONV_NORM_ROWS):
        y = y_ref[r0:r0 + CONV_NORM_ROWS, :] + cb_ref[...]
        y = _layer_norm(y, lg_ref[...], lb_ref[...])
        y = y * jax.nn.sigmoid(y)
        o_ref[0, r0:r0 + CONV_NORM_ROWS, :] = _rms_norm(y, og_ref[...]).astype(o_ref.dtype)


def _conv_branch(u, conv_w, conv_b, ln_g, ln_b, out_g):
    bsz, s, width = u.shape
    rows = 256
    halo_blocks = rows // CONV_HALO
    last_halo = s // CONV_HALO - 1
    vec = pl.BlockSpec((1, width), lambda b, j: (0, 0))
    blocks = (2 * _nbytes((CONV_HALO, width), F32) + _nbytes((rows, width), F32)
              + _nbytes((CONV_KERNEL, SUBLANES, width), F32) + _nbytes((rows, width), BF16))
    shifted = (SUBLANES, rows + 2 * CONV_HALO, width)
    scratch = _nbytes(shifted, F32) + _nbytes((rows, width), F32)
    w_tiles = jnp.broadcast_to(conv_w[:, None, :], (CONV_KERNEL, SUBLANES, width))
    return pl.pallas_call(
        functools.partial(_conv_kernel, rows=rows),
        grid=(bsz, s // rows),
        in_specs=[
            pl.BlockSpec((1, CONV_HALO, width), lambda b, j: (b, jnp.maximum(j * halo_blocks - 1, 0), 0)),
            pl.BlockSpec((1, rows, width), lambda b, j: (b, j, 0)),
            pl.BlockSpec((1, CONV_HALO, width), lambda b, j: (b, jnp.minimum((j + 1) * halo_blocks, last_halo), 0)),
            pl.BlockSpec((CONV_KERNEL, SUBLANES, width), lambda b, j: (0, 0, 0)),
            vec, vec, vec, vec,
        ],
        out_specs=pl.BlockSpec((1, rows, width), lambda b, j: (b, j, 0)),
        out_shape=jax.ShapeDtypeStruct((bsz, s, width), BF16),
        scratch_shapes=[pltpu.VMEM(shifted, F32), pltpu.VMEM((rows, width), F32)],
        compiler_params=_params(("parallel", "parallel"),
                                _vmem_limit(blocks, scratch=scratch, temps=6 * _nbytes((rows, width), F32))),
        name="conv_branch",
    )(u, u, u, w_tiles, conv_b.reshape(1, width), ln_g.reshape(1, width), ln_b.reshape(1, width),
      out_g.reshape(1, width))


OUT_SUBTILES = 2


def _out_kernel(at_ref, uc_ref, x_ref, w_ref, ag_ref, gate_ref, lg_ref, lb_ref, sc_ref, sh_ref, x_out, h_out,
                *, alpha):
    sub = at_ref.shape[1] // OUT_SUBTILES
    for r0 in range(0, at_ref.shape[1], sub):
        rows = pl.ds(r0, sub)
        attn = _rms_norm(at_ref[0, rows, :].astype(F32), ag_ref[...]).astype(BF16)
        mix = _dot(jnp.concatenate([attn, uc_ref[0, rows, :]], axis=-1), w_ref[...])
        x = _layer_norm(alpha * x_ref[0, rows, :] + (1.0 + gate_ref[0]) * mix, lg_ref[...], lb_ref[...])
        x_out[0, rows, :] = x
        h_out[0, rows, :] = (x * (1.0 + sc_ref[0]) + sh_ref[0]).astype(BF16)


def _out_proj(attn, uc, x, w_out, layer, attn_g, gate, ln_g, ln_b, sc, sh, alpha):
    bsz, s, d = x.shape
    aw = attn.shape[-1]
    cw = uc.shape[-1]
    tm = _tile(s, 512)

    def rows(width):
        return pl.BlockSpec((1, tm, width), lambda b, i: (b, i, 0))

    def vec(width):
        return pl.BlockSpec((1, width), lambda b, i: (0, 0))

    bvec = pl.BlockSpec((1, 1, d), lambda b, i: (b, 0, 0))
    blocks = (_nbytes((tm, aw), BF16) + _nbytes((tm, cw), BF16) + 2 * _nbytes((tm, d), F32)
              + _nbytes((tm, d), BF16) + _nbytes((d, d), BF16))
    return pl.pallas_call(
        functools.partial(_out_kernel, alpha=alpha),
        grid=(bsz, s // tm),
        in_specs=[rows(aw), rows(cw), rows(d), pl.BlockSpec((None, d, d), lambda b, i: (layer, 0, 0)), vec(aw), bvec,
                  vec(d), vec(d), bvec, bvec],
        out_specs=[rows(d), rows(d)],
        out_shape=[jax.ShapeDtypeStruct((bsz, s, d), F32), jax.ShapeDtypeStruct((bsz, s, d), BF16)],
        compiler_params=_params(("parallel", "parallel"), _vmem_limit(blocks, temps=4 * _nbytes((tm, d), F32))),
        name="out_proj",
    )(attn, uc, x, w_out, attn_g.reshape(1, aw), gate, ln_g.reshape(1, d), ln_b.reshape(1, d), sc, sh)


def _mlp_kernel(h_ref, w1_ref, b1_ref, w2_ref, b2_ref, x_ref, gate_ref, lg_ref, lb_ref, *rest, alpha, modulate):
    if modulate:
        sc_ref, sh_ref, x_out, h_out, acc_ref = rest
    else:
        x_out, acc_ref = rest
    t, f = pl.program_id(0), pl.program_id(1)
    n_tiles = pl.num_programs(0) - 1
    slot = t % 2

    def chunk():
        hid = jnp.maximum(_dot(h_ref[0], w1_ref[...]) + b1_ref[...], 0.0)
        return _dot((hid * hid).astype(BF16), w2_ref[...])

    def epilogue(src):
        ff = acc_ref[src] + b2_ref[...]
        x = _layer_norm(alpha * x_ref[0] + (1.0 + gate_ref[0]) * ff, lg_ref[...], lb_ref[...])
        x_out[0] = x
        if modulate:
            h_out[0] = (x * (1.0 + sc_ref[0]) + sh_ref[0]).astype(BF16)

    @pl.when((t == 0) & (f == 0))
    def _():
        acc_ref[1] = jnp.zeros(acc_ref.shape[1:], F32)

    for cur in range(2):
        mine = slot == cur

        @pl.when(mine & (f == 0) & (t < n_tiles))
        def _(cur=cur):
            acc_ref[cur] = chunk()
            epilogue(1 - cur)

        @pl.when(mine & (f > 0) & (t < n_tiles))
        def _(cur=cur):
            acc_ref[cur] += chunk()

        @pl.when(mine & (f == 0) & (t == n_tiles))
        def _(cur=cur):
            epilogue(1 - cur)


def _mlp(h, w1, b1, w2, b2, layer, x, gate, ln_g, ln_b, alpha, next_mod=None):
    bsz, s, d = x.shape
    ff_dim = w1.shape[-1]
    tm, tf = _tile(s, 512), _tile(ff_dim, 1024)
    per_batch = s // tm
    n_tiles, n_chunks = bsz * per_batch, ff_dim // tf
    modulate = next_mod is not None

    def prev(t):
        return jnp.maximum(t - 1, 0)

    cur_rows = pl.BlockSpec((1, tm, d), lambda t, f: (jnp.minimum(t, n_tiles - 1), 0, 0))
    prev_rows = pl.BlockSpec((1, tm, d), lambda t, f: (prev(t), 0, 0))
    prev_bvec = pl.BlockSpec((1, 1, d), lambda t, f: (prev(t) // per_batch, 0, 0))
    vec_d = pl.BlockSpec((1, d), lambda t, f: (0, 0))

    def chunk_of(t, f):
        return jnp.where(t == n_tiles, n_chunks - 1, f)

    in_specs = [
        cur_rows,
        pl.BlockSpec((None, d, tf), lambda t, f: (layer, 0, chunk_of(t, f))),
        pl.BlockSpec((1, tf), lambda t, f: (0, chunk_of(t, f))),
        pl.BlockSpec((None, tf, d), lambda t, f: (layer, chunk_of(t, f), 0)),
        vec_d, prev_rows, prev_bvec, vec_d, vec_d,
    ]
    tiles = (n_tiles, tm, d)
    args = [h.reshape(tiles), w1, b1.reshape(1, ff_dim), w2, b2.reshape(1, d), x.reshape(tiles), gate,
            ln_g.reshape(1, d), ln_b.reshape(1, d)]
    out_specs = [prev_rows]
    out_shape = [jax.ShapeDtypeStruct(tiles, F32)]
    blocks = (_nbytes((tm, d), BF16) + 2 * _nbytes((d, tf), BF16) + 2 * _nbytes((tm, d), F32))
    if modulate:
        in_specs += [prev_bvec, prev_bvec]
        args += list(next_mod)
        out_specs.append(prev_rows)
        out_shape.append(jax.ShapeDtypeStruct(tiles, BF16))
        blocks += _nbytes((tm, d), BF16)
    acc = (2, tm, d)
    out = pl.pallas_call(
        functools.partial(_mlp_kernel, alpha=alpha, modulate=modulate),
        grid=(n_tiles + 1, n_chunks),
        in_specs=in_specs,
        out_specs=out_specs,
        out_shape=out_shape,
        scratch_shapes=[pltpu.VMEM(acc, F32)],
        compiler_params=_params(("arbitrary", "arbitrary"),
                                _vmem_limit(blocks, scratch=_nbytes(acc, F32), temps=3 * _nbytes((tm, tf), F32))),
        name="mlp",
    )(*args)
    out = [o.reshape(bsz, s, -1) for o in out]
    return out if modulate else (out[0], None)


def _trunk(x, mods, emb_ln_g, emb_ln_b, w_in, conv_w, conv_b, conv_ln_g, conv_ln_b, attn_out_g, conv_out_g,
           w_out, ln1_g, ln1_b, w_mlp1, b_mlp1, w_mlp2, b_mlp2, ln2_g, ln2_b, bias):
    bsz, s, d = x.shape
    depth = w_in.shape[0]
    aw = d // 2
    cw = d - aw
    alpha = (2 * depth) ** 0.25

    def mod(layer, k):
        return mods[layer, :, k * d:(k + 1) * d].reshape(bsz, 1, d)

    x, h = _embed(x, emb_ln_g, emb_ln_b, mod(0, 1), mod(0, 0))
    for layer in range(depth):
        h2d = h.reshape(bsz * s, d)
        u = _glu_proj(h2d, w_in, layer, 3 * aw, 3 * aw + cw, cw).reshape(bsz, s, cw)
        qkv, y = _qkv_conv_proj(h, w_in, layer, 3 * aw, u, conv_w[layer])
        attn = _attention(qkv, bias, aw)
        uc = _conv_norm(y, conv_b[layer], conv_ln_g[layer], conv_ln_b[layer], conv_out_g[layer])
        x, h = _out_proj(attn, uc, x, w_out, layer, attn_out_g[layer], mod(layer, 2), ln1_g[layer],
                         ln1_b[layer], mod(layer, 4), mod(layer, 3), alpha)
        next_mod = (mod(layer + 1, 1), mod(layer + 1, 0)) if layer + 1 < depth else None
        x, h = _mlp(h, w_mlp1, b_mlp1[layer], w_mlp2, b_mlp2[layer], layer, x, mod(layer, 5),
                    ln2_g[layer], ln2_b[layer], alpha, next_mod)
    return x


def kernel(x_prompt, x_sample, c_prompt, c_sample, emb_ln_g, emb_ln_b, w_ada, b_ada, w_in, conv_w, conv_b,
           conv_ln_g, conv_ln_b, attn_out_g, conv_out_g, w_out, ln1_g, ln1_b, w_mlp1, b_mlp1, w_mlp2, b_mlp2,
           ln2_g, ln2_b):
    d = x_prompt.shape[-1]
    n_prompt, n_sample = c_prompt.shape[0], c_sample.shape[0]
    pad = (-(n_prompt + n_sample)) % 8
    c_all = jnp.concatenate([c_prompt, c_sample, jnp.zeros((pad, d), F32)], axis=0)
    mods = _ada_mod(c_all, w_ada, b_ada)
    n_heads = (d // 2) // HEAD_DIM
    bias = _bias_tiles(n_heads)
    weights = (emb_ln_g, emb_ln_b, w_in.astype(BF16), conv_w, conv_b, conv_ln_g, conv_ln_b, attn_out_g,
               conv_out_g, w_out.astype(BF16), ln1_g, ln1_b, w_mlp1.astype(BF16), b_mlp1, w_mlp2.astype(BF16),
               b_mlp2, ln2_g, ln2_b, bias)
    y_prompt = _trunk(x_prompt, mods[:, :n_prompt], *weights)
    y_sample = _trunk(x_sample, mods[:, n_prompt:n_prompt + n_sample], *weights)
    return (y_prompt, y_sample)
```

```python
import functools

import jax
import jax.numpy as jnp
from jax import lax
from jax.experimental import pallas as pl
from jax.experimental.pallas import tpu as pltpu

F32 = jnp.float32
BF16 = jnp.bfloat16

HEAD_DIM = 64
DILATED_BRANCHES = ((128, 1), (512, 4), (2048, 16))
HALF_WINDOW = 64
CONV_KERNEL = 31
CONV_PAD = (CONV_KERNEL - 1) // 2
N_MOD = 6
LN_EPS = 1e-5
NEG_INF = -1e30

LANES = 128
SUBLANES = 8
VMEM_BYTES_V7X = 64 * 1024 * 1024
VMEM_CAP = VMEM_BYTES_V7X - 8 * 1024 * 1024

Q_BLOCK = 128
K_WINDOW = Q_BLOCK + 2 * HALF_WINDOW
ATTN_UNROLL = 16
ATTN_GATHER_ROWS = 256


def _nbytes(shape, dtype):
    n = 1
    for s in shape:
        n *= s
    return n * jnp.dtype(dtype).itemsize


def _vmem_limit(pipelined, scratch=0, temps=0):
    return int(min(VMEM_CAP, 2 * pipelined + scratch + temps + (2 << 20)))


def _tile(n, preferred, align=LANES):
    t = min(preferred, n) // align * align
    while n % t:
        t -= align
    return t


def _params(semantics, vmem):
    return pltpu.CompilerParams(dimension_semantics=semantics, vmem_limit_bytes=vmem)


def _layer_norm(y, g, b):
    mu = jnp.mean(y, axis=-1, keepdims=True)
    yc = y - mu
    var = jnp.mean(yc * yc, axis=-1, keepdims=True)
    return yc * lax.rsqrt(var + LN_EPS) * g + b


def _rms_norm(y, g):
    return y * lax.rsqrt(jnp.mean(y * y, axis=-1, keepdims=True) + LN_EPS) * g


def _dot(a, b):
    return jnp.dot(a, b, preferred_element_type=F32)


def _ada_kernel(c_ref, w_ref, b_ref, o_ref):
    c = c_ref[...]
    a = c * jax.nn.sigmoid(c)
    w = w_ref[0]
    a_hi = a.astype(BF16)
    a_lo = (a - a_hi.astype(F32)).astype(BF16)
    w_hi = w.astype(BF16)
    w_lo = (w - w_hi.astype(F32)).astype(BF16)
    o_ref[0] = _dot(a_hi, w_hi) + _dot(a_hi, w_lo) + _dot(a_lo, w_hi) + b_ref[0]


def _ada_mod(c, w_ada, b_ada):
    depth, d, n = w_ada.shape
    rows = c.shape[0]
    tn = _tile(n, 1024)
    blocks = _nbytes((rows, d), F32) + _nbytes((d, tn), F32) + 2 * _nbytes((8, tn), F32)
    return pl.pallas_call(
        _ada_kernel,
        grid=(depth, n // tn),
        in_specs=[
            pl.BlockSpec((rows, d), lambda l, j: (0, 0)),
            pl.BlockSpec((1, d, tn), lambda l, j: (l, 0, j)),
            pl.BlockSpec((1, 1, tn), lambda l, j: (l, 0, j)),
        ],
        out_specs=pl.BlockSpec((1, rows, tn), lambda l, j: (l, 0, j)),
        out_shape=jax.ShapeDtypeStruct((depth, rows, n), F32),
        compiler_params=_params(("parallel", "parallel"), _vmem_limit(blocks, temps=2 * _nbytes((d, tn), F32))),
        name="ada_mod",
    )(c, w_ada, b_ada.reshape(depth, 1, n))


def _embed_kernel(x_ref, g_ref, b_ref, sc_ref, sh_ref, x_out, h_out):
    x = _layer_norm(x_ref[0], g_ref[...], b_ref[...])
    x_out[0] = x
    h_out[0] = (x * (1.0 + sc_ref[0]) + sh_ref[0]).astype(BF16)


def _embed(x, g, b, sc, sh):
    bsz, s, d = x.shape
    ts = 256
    row = pl.BlockSpec((1, ts, d), lambda i, j: (i, j, 0))
    vec = pl.BlockSpec((1, d), lambda i, j: (0, 0))
    bvec = pl.BlockSpec((1, 1, d), lambda i, j: (i, 0, 0))
    blocks = 2 * _nbytes((ts, d), F32) + _nbytes((ts, d), BF16)
    return pl.pallas_call(
        _embed_kernel,
        grid=(bsz, s // ts),
        in_specs=[row, vec, vec, bvec, bvec],
        out_specs=[row, row],
        out_shape=[jax.ShapeDtypeStruct((bsz, s, d), F32), jax.ShapeDtypeStruct((bsz, s, d), BF16)],
        compiler_params=_params(("parallel", "parallel"), _vmem_limit(blocks, temps=4 * _nbytes((ts, d), F32))),
        name="embed_ln",
    )(x, g.reshape(1, d), b.reshape(1, d), sc, sh)


CONV_HALO = 16
CONV_ROWS = 16
CONV_LANES = 256


def _qkv_conv_kernel(h_ref, w_ref, prev_ref, cur_ref, next_ref, taps_ref, o_ref, y_ref, sh_ref, *, tiles_per_seq):
    o_ref[...] = _dot(h_ref[...], w_ref[...])

    j = pl.program_id(1) % tiles_per_seq
    rows, lanes = cur_ref.shape[1], cur_ref.shape[2]
    zeros = jnp.zeros((CONV_HALO, lanes), F32)
    sh_ref[0, 0:CONV_HALO, :] = jnp.where(j > 0, prev_ref[0], zeros)
    sh_ref[0, CONV_HALO:CONV_HALO + rows, :] = cur_ref[0]
    sh_ref[0, CONV_HALO + rows:, :] = jnp.where(j < tiles_per_seq - 1, next_ref[0], zeros)
    shifted_rows = rows + 2 * CONV_HALO - SUBLANES
    for b in range(1, SUBLANES):
        sh_ref[b, 0:shifted_rows, :] = sh_ref[0, b:b + shifted_rows, :]

    slab_rows = CONV_ROWS + 2 * CONV_HALO - SUBLANES
    acc = None
    for r0 in range(0, rows, CONV_ROWS):
        if acc is None:
            acc = jnp.zeros((CONV_ROWS // SUBLANES, SUBLANES, lanes), F32)
        else:
            bits = lax.bitcast_convert_type(acc, jnp.uint32)
            acc = lax.shift_right_logical(lax.shift_right_logical(bits, jnp.uint32(16)), jnp.uint32(16)).astype(F32)
        for shift in range(SUBLANES):
            slab = sh_ref[shift, r0:r0 + slab_rows, :].reshape(slab_rows // SUBLANES, SUBLANES, lanes)
            for tap in range(CONV_KERNEL):
                offset = tap + (CONV_HALO - CONV_PAD)
                if offset % SUBLANES == shift:
                    first = offset // SUBLANES
                    acc = acc + slab[first:first + CONV_ROWS // SUBLANES] * taps_ref[tap][None]
        y_ref[0, r0:r0 + CONV_ROWS, :] = acc.reshape(CONV_ROWS, lanes)


def _qkv_conv_proj(h, w_in, layer, n_cols, u, conv_w):
    bsz, s, d = h.shape
    width = u.shape[-1]
    m = bsz * s
    tm = _tile(s, 1024)
    n_groups = width // CONV_LANES
    tn = n_cols // n_groups
    assert tn % LANES == 0 and width % CONV_LANES == 0
    tiles_per_seq = s // tm
    halo_blocks = tm // CONV_HALO
    last_halo = s // CONV_HALO - 1

    def seq_tile(i):
        return i // tiles_per_seq, i % tiles_per_seq

    def prev_map(n, i):
        b, j = seq_tile(i)
        return (b, jnp.maximum(j * halo_blocks - 1, 0), n)

    def cur_map(n, i):
        b, j = seq_tile(i)
        return (b, j, n)

    def next_map(n, i):
        b, j = seq_tile(i)
        return (b, jnp.minimum((j + 1) * halo_blocks, last_halo), n)

    taps = jnp.broadcast_to(conv_w[:, None, :], (CONV_KERNEL, SUBLANES, width))
    shifted = (SUBLANES, tm + 2 * CONV_HALO, CONV_LANES)
    blocks = (_nbytes((tm, d), BF16) + _nbytes((d, tn), BF16) + _nbytes((tm, tn), F32)
              + 2 * _nbytes((CONV_HALO, CONV_LANES), F32) + 2 * _nbytes((tm, CONV_LANES), F32)
              + _nbytes((CONV_KERNEL, SUBLANES, CONV_LANES), F32))
    qkv, y = pl.pallas_call(
        functools.partial(_qkv_conv_kernel, tiles_per_seq=tiles_per_seq),
        grid=(n_groups, m // tm),
        in_specs=[
            pl.BlockSpec((tm, d), lambda n, i: (i, 0)),
            pl.BlockSpec((None, d, tn), lambda n, i: (layer, 0, n)),
            pl.BlockSpec((1, CONV_HALO, CONV_LANES), prev_map),
            pl.BlockSpec((1, tm, CONV_LANES), cur_map),
            pl.BlockSpec((1, CONV_HALO, CONV_LANES), next_map),
            pl.BlockSpec((CONV_KERNEL, SUBLANES, CONV_LANES), lambda n, i: (0, 0, n)),
        ],
        out_specs=[pl.BlockSpec((tm, tn), lambda n, i: (i, n)),
                   pl.BlockSpec((1, tm, CONV_LANES), cur_map)],
        out_shape=[jax.ShapeDtypeStruct((m, n_cols), F32), jax.ShapeDtypeStruct((bsz, s, width), F32)],
        scratch_shapes=[pltpu.VMEM(shifted, F32)],
        compiler_params=_params(("parallel", "parallel"),
                                _vmem_limit(blocks, scratch=_nbytes(shifted, F32), temps=2 * _nbytes((tm, tn), F32))),
        name="qkv_conv_proj",
    )(h.reshape(m, d), w_in, u, u, u, taps)
    return qkv.reshape(bsz, s, n_cols), y


def _conv_norm_kernel(y_ref, cb_ref, lg_ref, lb_ref, og_ref, o_ref):
    y = _layer_norm(y_ref[0] + cb_ref[...], lg_ref[...], lb_ref[...])
    y = y * jax.nn.sigmoid(y)
    o_ref[0] = _rms_norm(y, og_ref[...]).astype(o_ref.dtype)


def _conv_norm(y, conv_b, ln_g, ln_b, out_g):
    bsz, s, width = y.shape
    ts = _tile(s, 512)
    row = pl.BlockSpec((1, ts, width), lambda b, j: (b, j, 0))
    vec = pl.BlockSpec((1, width), lambda b, j: (0, 0))
    blocks = _nbytes((ts, width), F32) + _nbytes((ts, width), BF16)
    return pl.pallas_call(
        _conv_norm_kernel,
        grid=(bsz, s // ts),
        in_specs=[row, vec, vec, vec, vec],
        out_specs=row,
        out_shape=jax.ShapeDtypeStruct((bsz, s, width), BF16),
        compiler_params=_params(("parallel", "parallel"), _vmem_limit(blocks, temps=4 * _nbytes((ts, width), F32))),
        name="conv_norm",
    )(y, conv_b.reshape(1, width), ln_g.reshape(1, width), ln_b.reshape(1, width), out_g.reshape(1, width))


def _glu_kernel(h_ref, wv_ref, wg_ref, o_ref):
    h = h_ref[...]
    o_ref[...] = _dot(h, wv_ref[...]) * jax.nn.sigmoid(_dot(h, wg_ref[...]))


def _glu_proj(h2d, w_in, layer, val_col, gate_col, width):
    m, d = h2d.shape
    tm, tn = _tile(m, 1024), _tile(width, 512)
    v_blk, g_blk = val_col // tn, gate_col // tn
    blocks = _nbytes((tm, d), BF16) + 2 * _nbytes((d, tn), BF16) + _nbytes((tm, tn), F32)
    return pl.pallas_call(
        _glu_kernel,
        grid=(width // tn, m // tm),
        in_specs=[
            pl.BlockSpec((tm, d), lambda n, i: (i, 0)),
            pl.BlockSpec((None, d, tn), lambda n, i: (layer, 0, v_blk + n)),
            pl.BlockSpec((None, d, tn), lambda n, i: (layer, 0, g_blk + n)),
        ],
        out_specs=pl.BlockSpec((tm, tn), lambda n, i: (i, n)),
        out_shape=jax.ShapeDtypeStruct((m, width), F32),
        compiler_params=_params(("parallel", "parallel"), _vmem_limit(blocks, temps=4 * _nbytes((tm, tn), F32))),
        name="glu_proj",
    )(h2d, w_in, w_in)


def _bias_tiles(n_heads):
    slopes = 2.0 ** (-8.0 * jnp.arange(1, n_heads + 1, dtype=F32) / n_heads)
    q = jnp.arange(Q_BLOCK)[:, None]
    c = jnp.arange(K_WINDOW)[None, :]
    branches = []
    for window, dilation in DILATED_BRANCHES:
        assert window == 2 * dilation * HALF_WINDOW
        tiles = []
        for key_start_minus_q_start in (0, -HALF_WINDOW, -2 * HALF_WINDOW):
            rel = key_start_minus_q_start + c - q
            dist = (dilation * jnp.abs(rel)).astype(F32)
            bias = -slopes[:, None, None] * dist[None]
            tiles.append(jnp.where((jnp.abs(rel) <= HALF_WINDOW)[None], bias, NEG_INF))
        branches.append(jnp.stack(tiles, axis=0))
    b = jnp.stack(branches, axis=0)
    b = b.reshape(len(DILATED_BRANCHES), 3, n_heads // 2, 2, Q_BLOCK, K_WINDOW)
    return b.transpose(2, 0, 1, 3, 4, 5)


def _attn_kernel(q_ref, k_ref, v_ref, bias_ref, o_ref, cq_ref, ck_ref, cv_ref, qs_ref, ks_ref, vs_ref,
                 acc_ref, m_ref, l_ref, stage_ref, *, seq):
    lane = lax.broadcasted_iota(jnp.int32, (1, LANES), 1)
    first_head = lane < HEAD_DIM
    (_, near), (_, mid), (_, far) = DILATED_BRANCHES
    sub = far // mid
    len_mid, len_far = seq // mid, seq // far
    gather_mid_rows = min(ATTN_GATHER_ROWS, len_mid)
    gather_far_rows = min(ATTN_GATHER_ROWS, len_far)

    def window(i, length):
        q0 = pl.multiple_of(i * Q_BLOCK, Q_BLOCK)
        k0 = pl.multiple_of(jnp.clip(q0 - HALF_WINDOW, 0, length - K_WINDOW), HALF_WINDOW)
        return q0, k0, (q0 - k0) // HALF_WINDOW

    def partial_softmax(q, kw, vw, branch, kind):
        tops, probs = [], []
        for head in range(2):
            mine = first_head if head == 0 else jnp.logical_not(first_head)
            qh = jnp.where(mine, q, jnp.zeros_like(q))
            s = lax.dot_general(qh, kw, (((1,), (1,)), ((), ())), preferred_element_type=F32)
            s = s + bias_ref[0, branch, kind, head]
            m = jnp.max(s, axis=-1, keepdims=True)
            tops.append(m)
            probs.append(jnp.exp(s - m).astype(BF16))
        zero = jnp.zeros_like(vw)
        ind = jnp.where(first_head, 1.0, 0.0).astype(BF16)
        rhs = jnp.concatenate(
            [jnp.concatenate([jnp.where(first_head, vw, zero), jnp.broadcast_to(ind, vw.shape)], axis=1),
             jnp.concatenate([jnp.where(first_head, zero, vw), jnp.broadcast_to(1 - ind, vw.shape)], axis=1)],
            axis=0)
        res = _dot(jnp.concatenate(probs, axis=1), rhs)
        return jnp.where(first_head, tops[0], tops[1]), res[:, LANES:], res[:, :LANES]

    def merge(m_old, l_old, o_old, m_blk, l_blk, o_blk):
        m_new = jnp.maximum(m_old, m_blk)
        w_old = jnp.exp(m_old - m_new)
        w_blk = jnp.exp(m_blk - m_new)
        return m_new, w_old * l_old + w_blk * l_blk, w_old * o_old + w_blk * o_blk

    def grouped(n_blocks, block):
        unroll = _tile(n_blocks, ATTN_UNROLL, align=1)

        def group(g, carry):
            for u in range(unroll):
                block(g * unroll + u, u)
            return carry

        lax.fori_loop(0, n_blocks // unroll, group, 0)

    def mid_class(r, carry):
        base = pl.multiple_of(r * len_mid, len_mid)

        def gather_mid(c, carry2):
            c0 = pl.multiple_of(c * gather_mid_rows, gather_mid_rows)
            src = pl.ds(r + mid * c0, gather_mid_rows, stride=mid)
            dst = pl.ds(c0, gather_mid_rows)
            x = q_ref[0, src, :]
            cq_ref[dst, :] = x
            qs_ref[dst, :] = x.astype(BF16)
            x = k_ref[0, src, :]
            ck_ref[dst, :] = x
            ks_ref[dst, :] = x.astype(BF16)
            x = v_ref[0, src, :]
            cv_ref[dst, :] = x
            vs_ref[dst, :] = x.astype(BF16)
            return carry2

        lax.fori_loop(0, len_mid // gather_mid_rows, gather_mid, 0)

        def mid_block(i, slot):
            q0, k0, kind = window(i, len_mid)
            m, l, o = partial_softmax(qs_ref[pl.ds(q0, Q_BLOCK), :], ks_ref[pl.ds(k0, K_WINDOW), :],
                                      vs_ref[pl.ds(k0, K_WINDOW), :], 1, kind)
            rows = pl.ds(base + q0, Q_BLOCK)
            m_ref[rows, :] = m
            l_ref[rows, :] = l
            acc_ref[rows, :] = o

        grouped(len_mid // Q_BLOCK, mid_block)

        for c in range(sub):
            def gather_far(g, carry2, c=c):
                g0 = pl.multiple_of(g * gather_far_rows, gather_far_rows)
                src = pl.ds(c + sub * g0, gather_far_rows, stride=sub)
                dst = pl.ds(c * len_far + g0, gather_far_rows)
                qs_ref[dst, :] = cq_ref[src, :].astype(BF16)
                ks_ref[dst, :] = ck_ref[src, :].astype(BF16)
                vs_ref[dst, :] = cv_ref[src, :].astype(BF16)
                return carry2

            lax.fori_loop(0, len_far // gather_far_rows, gather_far, 0)

        far_blocks = len_far // Q_BLOCK

        def far_block(j, slot):
            c = j // far_blocks
            q0, k0, kind = window(j - c * far_blocks, len_far)
            off = pl.multiple_of(c * len_far, len_far)
            m, l, o = partial_softmax(qs_ref[pl.ds(off + q0, Q_BLOCK), :], ks_ref[pl.ds(off + k0, K_WINDOW), :],
                                      vs_ref[pl.ds(off + k0, K_WINDOW), :], 2, kind)
            rows = pl.ds(base + c + sub * q0, Q_BLOCK, stride=sub)
            m, l, o = merge(m_ref[rows, :], l_ref[rows, :], acc_ref[rows, :], m, l, o)
            m_ref[rows, :] = m
            l_ref[rows, :] = l
            acc_ref[rows, :] = o

        grouped(sub * far_blocks, far_block)
        return carry

    lax.fori_loop(0, mid, mid_class, 0)

    per_class = Q_BLOCK // mid

    def near_block(i, slot):
        q0, k0, kind = window(i, seq)
        m, l, o = partial_softmax(q_ref[0, pl.ds(q0, Q_BLOCK), :].astype(BF16),
                                  k_ref[0, pl.ds(k0, K_WINDOW), :].astype(BF16),
                                  v_ref[0, pl.ds(k0, K_WINDOW), :].astype(BF16), 0, kind)
        j0 = pl.multiple_of(i * per_class, per_class)
        for c in range(mid):
            src = pl.ds(c * len_mid + j0, per_class)
            dst = pl.ds(c, per_class, stride=mid)
            stage_ref[slot, 0, dst, :] = m_ref[src, :]
            stage_ref[slot, 1, dst, :] = l_ref[src, :]
            stage_ref[slot, 2, dst, :] = acc_ref[src, :]
        m, l, o = merge(stage_ref[slot, 0], stage_ref[slot, 1], stage_ref[slot, 2], m, l, o)
        o_ref[0, pl.ds(q0, Q_BLOCK), :] = (o / l).astype(o_ref.dtype)

    grouped(seq // Q_BLOCK, near_block)


def _attention(qkv, bias, attn_width):
    bsz, seq, _ = qkv.shape
    n_pairs = attn_width // LANES
    (_, near), (_, mid), (_, far) = DILATED_BRANCHES
    assert near == 1 and far % mid == 0 and Q_BLOCK % mid == 0
    assert seq % (far * Q_BLOCK) == 0 and seq // far >= K_WINDOW

    def slab(col0):
        return pl.BlockSpec((1, seq, LANES), lambda p, b: (b, 0, col0 + p))

    bias_block = (1,) + bias.shape[1:]
    mid_rows = (seq // mid, LANES)
    stage = (ATTN_UNROLL, 3, Q_BLOCK, LANES)
    blocks = 3 * _nbytes((seq, LANES), F32) + _nbytes(bias_block, F32) + _nbytes((seq, LANES), BF16)
    scratch = (3 * _nbytes(mid_rows, F32) + 3 * _nbytes(mid_rows, BF16) + 3 * _nbytes((seq, LANES), F32)
               + _nbytes(stage, F32))
    return pl.pallas_call(
        functools.partial(_attn_kernel, seq=seq),
        grid=(n_pairs, bsz),
        in_specs=[slab(0), slab(n_pairs), slab(2 * n_pairs),
                  pl.BlockSpec(bias_block, lambda p, b: (p, 0, 0, 0, 0, 0))],
        out_specs=pl.BlockSpec((1, seq, LANES), lambda p, b: (b, 0, p)),
        out_shape=jax.ShapeDtypeStruct((bsz, seq, attn_width), BF16),
        scratch_shapes=([pltpu.VMEM(mid_rows, F32)] * 3 + [pltpu.VMEM(mid_rows, BF16)] * 3
                        + [pltpu.VMEM((seq, LANES), F32)] * 3 + [pltpu.VMEM(stage, F32)]),
        compiler_params=_params(("parallel", "parallel"),
                                _vmem_limit(blocks, scratch=scratch, temps=16 * _nbytes((Q_BLOCK, K_WINDOW), F32))),
        name="attention",
    )(qkv, qkv, qkv, bias)


OUT_SUBTILES = 2


def _out_kernel(at_ref, uc_ref, x_ref, w_ref, ag_ref, gate_ref, lg_ref, lb_ref, sc_ref, sh_ref, x_out, h_out,
                *, alpha):
    sub = at_ref.shape[1] // OUT_SUBTILES
    for r0 in range(0, at_ref.shape[1], sub):
        rows = pl.ds(r0, sub)
        attn = _rms_norm(at_ref[0, rows, :].astype(F32), ag_ref[...]).astype(BF16)
        mix = _dot(jnp.concatenate([attn, uc_ref[0, rows, :]], axis=-1), w_ref[...])
        x = _layer_norm(alpha * x_ref[0, rows, :] + (1.0 + gate_ref[0]) * mix, lg_ref[...], lb_ref[...])
        x_out[0, rows, :] = x
        h_out[0, rows, :] = (x * (1.0 + sc_ref[0]) + sh_ref[0]).astype(BF16)


def _out_proj(attn, uc, x, w_out, layer, attn_g, gate, ln_g, ln_b, sc, sh, alpha):
    bsz, s, d = x.shape
    aw = attn.shape[-1]
    cw = uc.shape[-1]
    tm = _tile(s, 512)

    def rows(width):
        return pl.BlockSpec((1, tm, width), lambda b, i: (b, i, 0))

    def vec(width):
        return pl.BlockSpec((1, width), lambda b, i: (0, 0))

    bvec = pl.BlockSpec((1, 1, d), lambda b, i: (b, 0, 0))
    blocks = (_nbytes((tm, aw), BF16) + _nbytes((tm, cw), BF16) + 2 * _nbytes((tm, d), F32)
              + _nbytes((tm, d), BF16) + _nbytes((d, d), BF16))
    return pl.pallas_call(
        functools.partial(_out_kernel, alpha=alpha),
        grid=(bsz, s // tm),
        in_specs=[rows(aw), rows(cw), rows(d), pl.BlockSpec((None, d, d), lambda b, i: (layer, 0, 0)), vec(aw), bvec,
                  vec(d), vec(d), bvec, bvec],
        out_specs=[rows(d), rows(d)],
        out_shape=[jax.ShapeDtypeStruct((bsz, s, d), F32), jax.ShapeDtypeStruct((bsz, s, d), BF16)],
        compiler_params=_params(("parallel", "parallel"), _vmem_limit(blocks, temps=4 * _nbytes((tm, d), F32))),
        name="out_proj",
    )(attn, uc, x, w_out, attn_g.reshape(1, aw), gate, ln_g.reshape(1, d), ln_b.reshape(1, d), sc, sh)


def _mlp_kernel(h_ref, w1_ref, b1_ref, w2_ref, b2_ref, x_ref, gate_ref, lg_ref, lb_ref, *rest, alpha, modulate):
    if modulate:
        sc_ref, sh_ref, x_out, h_out, acc_ref = rest
    else:
        x_out, acc_ref = rest
    t, f = pl.program_id(0), pl.program_id(1)
    n_tiles = pl.num_programs(0) - 1
    slot = t % 2

    def chunk():
        hid = jnp.maximum(_dot(h_ref[0], w1_ref[...]) + b1_ref[...], 0.0)
        return _dot((hid * hid).astype(BF16), w2_ref[...])

    def epilogue(src):
        ff = acc_ref[src] + b2_ref[...]
        x = _layer_norm(alpha * x_ref[0] + (1.0 + gate_ref[0]) * ff, lg_ref[...], lb_ref[...])
        x_out[0] = x
        if modulate:
            h_out[0] = (x * (1.0 + sc_ref[0]) + sh_ref[0]).astype(BF16)

    @pl.when((t == 0) & (f == 0))
    def _():
        acc_ref[1] = jnp.zeros(acc_ref.shape[1:], F32)

    for cur in range(2):
        mine = slot == cur

        @pl.when(mine & (f == 0) & (t < n_tiles))
        def _(cur=cur):
            acc_ref[cur] = chunk()
            epilogue(1 - cur)

        @pl.when(mine & (f > 0) & (t < n_tiles))
        def _(cur=cur):
            acc_ref[cur] += chunk()

        @pl.when(mine & (f == 0) & (t == n_tiles))
        def _(cur=cur):
            epilogue(1 - cur)


def _mlp(h, w1, b1, w2, b2, layer, x, gate, ln_g, ln_b, alpha, next_mod=None):
    bsz, s, d = x.shape
    ff_dim = w1.shape[-1]
    tm, tf = _tile(s, 512), _tile(ff_dim, 1024)
    per_batch = s // tm
    n_tiles, n_chunks = bsz * per_batch, ff_dim // tf
    modulate = next_mod is not None

    def prev(t):
        return jnp.maximum(t - 1, 0)

    cur_rows = pl.BlockSpec((1, tm, d), lambda t, f: (jnp.minimum(t, n_tiles - 1), 0, 0))
    prev_rows = pl.BlockSpec((1, tm, d), lambda t, f: (prev(t), 0, 0))
    prev_bvec = pl.BlockSpec((1, 1, d), lambda t, f: (prev(t) // per_batch, 0, 0))
    vec_d = pl.BlockSpec((1, d), lambda t, f: (0, 0))

    def chunk_of(t, f):
        return jnp.where(t == n_tiles, n_chunks - 1, f)

    in_specs = [
        cur_rows,
        pl.BlockSpec((None, d, tf), lambda t, f: (layer, 0, chunk_of(t, f))),
        pl.BlockSpec((1, tf), lambda t, f: (0, chunk_of(t, f))),
        pl.BlockSpec((None, tf, d), lambda t, f: (layer, chunk_of(t, f), 0)),
        vec_d, prev_rows, prev_bvec, vec_d, vec_d,
    ]
    tiles = (n_tiles, tm, d)
    args = [h.reshape(tiles), w1, b1.reshape(1, ff_dim), w2, b2.reshape(1, d), x.reshape(tiles), gate,
            ln_g.reshape(1, d), ln_b.reshape(1, d)]
    out_specs = [prev_rows]
    out_shape = [jax.ShapeDtypeStruct(tiles, F32)]
    blocks = (_nbytes((tm, d), BF16) + 2 * _nbytes((d, tf), BF16) + 2 * _nbytes((tm, d), F32))
    if modulate:
        in_specs += [prev_bvec, prev_bvec]
        args += list(next_mod)
        out_specs.append(prev_rows)
        out_shape.append(jax.ShapeDtypeStruct(tiles, BF16))
        blocks += _nbytes((tm, d), BF16)
    acc = (2, tm, d)
    out = pl.pallas_call(
        functools.partial(_mlp_kernel, alpha=alpha, modulate=modulate),
        grid=(n_tiles + 1, n_chunks),
        in_specs=in_specs,
        out_specs=out_specs,
        out_shape=out_shape,
        scratch_shapes=[pltpu.VMEM(acc, F32)],
        compiler_params=_params(("arbitrary", "arbitrary"),
                                _vmem_limit(blocks, scratch=_nbytes(acc, F32), temps=3 * _nbytes((tm, tf), F32))),
        name="mlp",
    )(*args)
    out = [o.reshape(bsz, s, -1) for o in out]
    return out if modulate else (out[0], None)


def _trunk(x, mods, emb_ln_g, emb_ln_b, w_in, conv_w, conv_b, conv_ln_g, conv_ln_b, attn_out_g, conv_out_g,
           w_out, ln1_g, ln1_b, w_mlp1, b_mlp1, w_mlp2, b_mlp2, ln2_g, ln2_b, bias):
    bsz, s, d = x.shape
    depth = w_in.shape[0]
    aw = d // 2
    cw = d - aw
    alpha = (2 * depth) ** 0.25

    def mod(layer, k):
        return mods[layer, :, k * d:(k + 1) * d].reshape(bsz, 1, d)

    x, h = _embed(x, emb_ln_g, emb_ln_b, mod(0, 1), mod(0, 0))
    for layer in range(depth):
        h2d = h.reshape(bsz * s, d)
        u = _glu_proj(h2d, w_in, layer, 3 * aw, 3 * aw + cw, cw).reshape(bsz, s, cw)
        qkv, y = _qkv_conv_proj(h, w_in, layer, 3 * aw, u, conv_w[layer])
        attn = _attention(qkv, bias, aw)
        uc = _conv_norm(y, conv_b[layer], conv_ln_g[layer], conv_ln_b[layer], conv_out_g[layer])
        x, h = _out_proj(attn, uc, x, w_out, layer, attn_out_g[layer], mod(layer, 2), ln1_g[layer],
                         ln1_b[layer], mod(layer, 4), mod(layer, 3), alpha)
        next_mod = (mod(layer + 1, 1), mod(layer + 1, 0)) if layer + 1 < depth else None
        x, h = _mlp(h, w_mlp1, b_mlp1[layer], w_mlp2, b_mlp2[layer], layer, x, mod(layer, 5),
                    ln2_g[layer], ln2_b[layer], alpha, next_mod)
    return x


def kernel(x_prompt, x_sample, c_prompt, c_sample, emb_ln_g, emb_ln_b, w_ada, b_ada, w_in, conv_w, conv_b,
           conv_ln_g, conv_ln_b, attn_out_g, conv_out_g, w_out, ln1_g, ln1_b, w_mlp1, b_mlp1, w_mlp2, b_mlp2,
           ln2_g, ln2_b):
    d = x_prompt.shape[-1]
    n_prompt, n_sample = c_prompt.shape[0], c_sample.shape[0]
    pad = (-(n_prompt + n_sample)) % 8
    c_all = jnp.concatenate([c_prompt, c_sample, jnp.zeros((pad, d), F32)], axis=0)
    mods = _ada_mod(c_all, w_ada, b_ada)
    n_heads = (d // 2) // HEAD_DIM
    bias = _bias_tiles(n_heads)
    q_scale = jnp.where(jnp.arange(w_in.shape[-1]) < d // 2, HEAD_DIM ** -0.5, 1.0).astype(F32)
    weights = (emb_ln_g, emb_ln_b, (w_in * q_scale).astype(BF16), conv_w, conv_b, conv_ln_g, conv_ln_b, attn_out_g,
               conv_out_g, w_out.astype(BF16), ln1_g, ln1_b, w_mlp1.astype(BF16), b_mlp1, w_mlp2.astype(BF16),
               b_mlp2, ln2_g, ln2_b, bias)
    y_prompt = _trunk(x_prompt, mods[:, :n_prompt], *weights)
    y_sample = _trunk(x_sample, mods[:, n_prompt:n_prompt + n_sample], *weights)
    return (y_prompt, y_sample)
```

```python
import functools

import jax
import jax.numpy as jnp
from jax import lax
from jax.experimental import pallas as pl
from jax.experimental.pallas import tpu as pltpu

F32 = jnp.float32
BF16 = jnp.bfloat16

HEAD_DIM = 64
DILATED_BRANCHES = ((128, 1), (512, 4), (2048, 16))
HALF_WINDOW = 64
CONV_KERNEL = 31
CONV_PAD = (CONV_KERNEL - 1) // 2
LN_EPS = 1e-5
NEG_INF = -1e30

LANES = 128
SUBLANES = 8
VMEM_BYTES_V7X = 64 * 1024 * 1024
VMEM_CAP = VMEM_BYTES_V7X - 8 * 1024 * 1024

Q_BLOCK = 128
K_WINDOW = Q_BLOCK + 2 * HALF_WINDOW
ATTN_UNROLL = 16
ATTN_GATHER_ROWS = 256


def _nbytes(shape, dtype):
    n = 1
    for s in shape:
        n *= s
    return n * jnp.dtype(dtype).itemsize


def _vmem_limit(pipelined, scratch=0, temps=0):
    return int(min(VMEM_CAP, 2 * pipelined + scratch + temps + (2 << 20)))


def _tile(n, preferred, align=LANES):
    t = min(preferred, n) // align * align
    while n % t:
        t -= align
    return t


def _params(semantics, vmem):
    return pltpu.CompilerParams(dimension_semantics=semantics, vmem_limit_bytes=vmem)


def _layer_norm(y, g, b):
    mu = jnp.mean(y, axis=-1, keepdims=True)
    yc = y - mu
    var = jnp.mean(yc * yc, axis=-1, keepdims=True)
    return yc * lax.rsqrt(var + LN_EPS) * g + b


def _rms_norm(y, g):
    return y * lax.rsqrt(jnp.mean(y * y, axis=-1, keepdims=True) + LN_EPS) * g


def _dot(a, b):
    return jnp.dot(a, b, preferred_element_type=F32)


def _ada_kernel(c_ref, w_ref, b_ref, o_ref):
    c = c_ref[...]
    a = c * jax.nn.sigmoid(c)
    w = w_ref[0]
    a_hi = a.astype(BF16)
    a_lo = (a - a_hi.astype(F32)).astype(BF16)
    w_hi = w.astype(BF16)
    w_lo = (w - w_hi.astype(F32)).astype(BF16)
    o_ref[0] = _dot(a_hi, w_hi) + _dot(a_hi, w_lo) + _dot(a_lo, w_hi) + b_ref[0]


def _ada_mod(c, w_ada, b_ada):
    depth, d, n = w_ada.shape
    rows = c.shape[0]
    tn = _tile(n, 1024)
    blocks = _nbytes((rows, d), F32) + _nbytes((d, tn), F32) + 2 * _nbytes((8, tn), F32)
    return pl.pallas_call(
        _ada_kernel,
        grid=(depth, n // tn),
        in_specs=[
            pl.BlockSpec((rows, d), lambda l, j: (0, 0)),
            pl.BlockSpec((1, d, tn), lambda l, j: (l, 0, j)),
            pl.BlockSpec((1, 1, tn), lambda l, j: (l, 0, j)),
        ],
        out_specs=pl.BlockSpec((1, rows, tn), lambda l, j: (l, 0, j)),
        out_shape=jax.ShapeDtypeStruct((depth, rows, n), F32),
        compiler_params=_params(("parallel", "parallel"), _vmem_limit(blocks, temps=2 * _nbytes((d, tn), F32))),
        name="ada_mod",
    )(c, w_ada, b_ada.reshape(depth, 1, n))


def _embed_kernel(x_ref, g_ref, b_ref, sc_ref, sh_ref, x_out, h_out):
    x = _layer_norm(x_ref[0], g_ref[...], b_ref[...])
    x_out[0] = x
    h_out[0] = (x * (1.0 + sc_ref[0]) + sh_ref[0]).astype(BF16)


def _embed(x, g, b, sc, sh):
    bsz, s, d = x.shape
    ts = _tile(s, 512)
    row = pl.BlockSpec((1, ts, d), lambda i, j: (i, j, 0))
    vec = pl.BlockSpec((1, d), lambda i, j: (0, 0))
    bvec = pl.BlockSpec((1, 1, d), lambda i, j: (i, 0, 0))
    blocks = 2 * _nbytes((ts, d), F32) + _nbytes((ts, d), BF16)
    return pl.pallas_call(
        _embed_kernel,
        grid=(bsz, s // ts),
        in_specs=[row, vec, vec, bvec, bvec],
        out_specs=[row, row],
        out_shape=[jax.ShapeDtypeStruct((bsz, s, d), F32), jax.ShapeDtypeStruct((bsz, s, d), BF16)],
        compiler_params=_params(("parallel", "parallel"), _vmem_limit(blocks, temps=4 * _nbytes((ts, d), F32))),
        name="embed_ln",
    )(x, g.reshape(1, d), b.reshape(1, d), sc, sh)


CONV_HALO = 16
CONV_ROWS = 16
CONV_LANES = 256


def _qkv_conv_kernel(h_ref, w_ref, prev_ref, cur_ref, next_ref, taps_ref, o_ref, y_ref, sh_ref, *, tiles_per_seq):
    o_ref[...] = _dot(h_ref[...], w_ref[...])

    j = pl.program_id(1) % tiles_per_seq
    rows, lanes = cur_ref.shape[1], cur_ref.shape[2]
    zeros = jnp.zeros((CONV_HALO, lanes), F32)
    sh_ref[0, 0:CONV_HALO, :] = jnp.where(j > 0, prev_ref[0], zeros)
    sh_ref[0, CONV_HALO:CONV_HALO + rows, :] = cur_ref[0]
    sh_ref[0, CONV_HALO + rows:, :] = jnp.where(j < tiles_per_seq - 1, next_ref[0], zeros)
    shifted_rows = rows + 2 * CONV_HALO - SUBLANES
    for b in range(1, SUBLANES):
        sh_ref[b, 0:shifted_rows, :] = sh_ref[0, b:b + shifted_rows, :]

    slab_rows = CONV_ROWS + 2 * CONV_HALO - SUBLANES
    acc = None
    for r0 in range(0, rows, CONV_ROWS):
        if acc is None:
            acc = jnp.zeros((CONV_ROWS // SUBLANES, SUBLANES, lanes), F32)
        else:
            bits = lax.bitcast_convert_type(acc, jnp.uint32)
            acc = lax.shift_right_logical(lax.shift_right_logical(bits, jnp.uint32(16)), jnp.uint32(16)).astype(F32)
        for shift in range(SUBLANES):
            slab = sh_ref[shift, r0:r0 + slab_rows, :].reshape(slab_rows // SUBLANES, SUBLANES, lanes)
            for tap in range(CONV_KERNEL):
                offset = tap + (CONV_HALO - CONV_PAD)
                if offset % SUBLANES == shift:
                    first = offset // SUBLANES
                    acc = acc + slab[first:first + CONV_ROWS // SUBLANES] * taps_ref[tap][None]
        y_ref[0, r0:r0 + CONV_ROWS, :] = acc.reshape(CONV_ROWS, lanes)


def _qkv_conv_proj(h, w_in, layer, n_cols, u, conv_w):
    bsz, s, d = h.shape
    width = u.shape[-1]
    m = bsz * s
    tm = _tile(s, 1024)
    n_groups = width // CONV_LANES
    tn = n_cols // n_groups
    assert tn % LANES == 0 and width % CONV_LANES == 0
    tiles_per_seq = s // tm
    halo_blocks = tm // CONV_HALO
    last_halo = s // CONV_HALO - 1

    def seq_tile(i):
        return i // tiles_per_seq, i % tiles_per_seq

    def prev_map(n, i):
        b, j = seq_tile(i)
        return (b, jnp.maximum(j * halo_blocks - 1, 0), n)

    def cur_map(n, i):
        b, j = seq_tile(i)
        return (b, j, n)

    def next_map(n, i):
        b, j = seq_tile(i)
        return (b, jnp.minimum((j + 1) * halo_blocks, last_halo), n)

    taps = jnp.broadcast_to(conv_w[:, None, :], (CONV_KERNEL, SUBLANES, width))
    shifted = (SUBLANES, tm + 2 * CONV_HALO, CONV_LANES)
    blocks = (_nbytes((tm, d), BF16) + _nbytes((d, tn), BF16) + _nbytes((tm, tn), F32)
              + 2 * _nbytes((CONV_HALO, CONV_LANES), F32) + 2 * _nbytes((tm, CONV_LANES), F32)
              + _nbytes((CONV_KERNEL, SUBLANES, CONV_LANES), F32))
    qkv, y = pl.pallas_call(
        functools.partial(_qkv_conv_kernel, tiles_per_seq=tiles_per_seq),
        grid=(n_groups, m // tm),
        in_specs=[
            pl.BlockSpec((tm, d), lambda n, i: (i, 0)),
            pl.BlockSpec((None, d, tn), lambda n, i: (layer, 0, n)),
            pl.BlockSpec((1, CONV_HALO, CONV_LANES), prev_map),
            pl.BlockSpec((1, tm, CONV_LANES), cur_map),
            pl.BlockSpec((1, CONV_HALO, CONV_LANES), next_map),
            pl.BlockSpec((CONV_KERNEL, SUBLANES, CONV_LANES), lambda n, i: (0, 0, n)),
        ],
        out_specs=[pl.BlockSpec((tm, tn), lambda n, i: (i, n)),
                   pl.BlockSpec((1, tm, CONV_LANES), cur_map)],
        out_shape=[jax.ShapeDtypeStruct((m, n_cols), F32), jax.ShapeDtypeStruct((bsz, s, width), F32)],
        scratch_shapes=[pltpu.VMEM(shifted, F32)],
        compiler_params=_params(("parallel", "parallel"),
                                _vmem_limit(blocks, scratch=_nbytes(shifted, F32), temps=2 * _nbytes((tm, tn), F32))),
        name="qkv_conv_proj",
    )(h.reshape(m, d), w_in, u, u, u, taps)
    return qkv.reshape(bsz, s, n_cols), y


def _conv_norm_kernel(y_ref, cb_ref, lg_ref, lb_ref, og_ref, o_ref):
    y = _layer_norm(y_ref[0] + cb_ref[...], lg_ref[...], lb_ref[...])
    y = y * jax.nn.sigmoid(y)
    o_ref[0] = _rms_norm(y, og_ref[...]).astype(o_ref.dtype)


def _conv_norm(y, conv_b, ln_g, ln_b, out_g):
    bsz, s, width = y.shape
    ts = _tile(s, 512)
    row = pl.BlockSpec((1, ts, width), lambda b, j: (b, j, 0))
    vec = pl.BlockSpec((1, width), lambda b, j: (0, 0))
    blocks = _nbytes((ts, width), F32) + _nbytes((ts, width), BF16)
    return pl.pallas_call(
        _conv_norm_kernel,
        grid=(bsz, s // ts),
        in_specs=[row, vec, vec, vec, vec],
        out_specs=row,
        out_shape=jax.ShapeDtypeStruct((bsz, s, width), BF16),
        compiler_params=_params(("parallel", "parallel"), _vmem_limit(blocks, temps=4 * _nbytes((ts, width), F32))),
        name="conv_norm",
    )(y, conv_b.reshape(1, width), ln_g.reshape(1, width), ln_b.reshape(1, width), out_g.reshape(1, width))


def _glu_kernel(h_ref, wv_ref, wg_ref, o_ref):
    h = h_ref[...]
    o_ref[...] = _dot(h, wv_ref[...]) * jax.nn.sigmoid(_dot(h, wg_ref[...]))


def _glu_proj(h2d, w_in, layer, val_col, gate_col, width):
    m, d = h2d.shape
    tm, tn = _tile(m, 1024), _tile(width, 512)
    v_blk, g_blk = val_col // tn, gate_col // tn
    blocks = _nbytes((tm, d), BF16) + 2 * _nbytes((d, tn), BF16) + _nbytes((tm, tn), F32)
    return pl.pallas_call(
        _glu_kernel,
        grid=(width // tn, m // tm),
        in_specs=[
            pl.BlockSpec((tm, d), lambda n, i: (i, 0)),
            pl.BlockSpec((None, d, tn), lambda n, i: (layer, 0, v_blk + n)),
            pl.BlockSpec((None, d, tn), lambda n, i: (layer, 0, g_blk + n)),
        ],
        out_specs=pl.BlockSpec((tm, tn), lambda n, i: (i, n)),
        out_shape=jax.ShapeDtypeStruct((m, width), F32),
        compiler_params=_params(("parallel", "parallel"), _vmem_limit(blocks, temps=4 * _nbytes((tm, tn), F32))),
        name="glu_proj",
    )(h2d, w_in, w_in)


def _bias_tiles(n_heads):
    slopes = 2.0 ** (-8.0 * jnp.arange(1, n_heads + 1, dtype=F32) / n_heads)
    q = jnp.arange(Q_BLOCK)[:, None]
    c = jnp.arange(K_WINDOW)[None, :]
    branches = []
    for window, dilation in DILATED_BRANCHES:
        assert window == 2 * dilation * HALF_WINDOW
        tiles = []
        for key_start_minus_q_start in (0, -HALF_WINDOW, -2 * HALF_WINDOW):
            rel = key_start_minus_q_start + c - q
            dist = (dilation * jnp.abs(rel)).astype(F32)
            bias = -slopes[:, None, None] * dist[None]
            tiles.append(jnp.where((jnp.abs(rel) <= HALF_WINDOW)[None], bias, NEG_INF))
        branches.append(jnp.stack(tiles, axis=0))
    b = jnp.stack(branches, axis=0)
    b = b.reshape(len(DILATED_BRANCHES), 3, n_heads // 2, 2, Q_BLOCK, K_WINDOW)
    return b.transpose(2, 0, 1, 3, 4, 5)


def _attn_kernel(q_ref, k_ref, v_ref, bias_ref, o_ref, cq_ref, ck_ref, cv_ref, qs_ref, ks_ref, vs_ref,
                 acc_ref, m_ref, l_ref, stage_ref, *, seq):
    lane = lax.broadcasted_iota(jnp.int32, (1, LANES), 1)
    first_head = lane < HEAD_DIM
    _, (_, mid), (_, far) = DILATED_BRANCHES
    sub = far // mid
    len_mid, len_far = seq // mid, seq // far
    gather_mid_rows = min(ATTN_GATHER_ROWS, len_mid)
    gather_far_rows = min(ATTN_GATHER_ROWS, len_far)

    def window(i, length):
        q0 = pl.multiple_of(i * Q_BLOCK, Q_BLOCK)
        k0 = pl.multiple_of(jnp.clip(q0 - HALF_WINDOW, 0, length - K_WINDOW), HALF_WINDOW)
        return q0, k0, (q0 - k0) // HALF_WINDOW

    def partial_softmax(q, kw, vw, branch, kind):
        tops, probs = [], []
        for head in range(2):
            mine = first_head if head == 0 else jnp.logical_not(first_head)
            qh = jnp.where(mine, q, jnp.zeros_like(q))
            s = lax.dot_general(qh, kw, (((1,), (1,)), ((), ())), preferred_element_type=F32)
            s = s + bias_ref[0, branch, kind, head]
            m = jnp.max(s, axis=-1, keepdims=True)
            tops.append(m)
            probs.append(jnp.exp(s - m).astype(BF16))
        zero = jnp.zeros_like(vw)
        ind = jnp.where(first_head, 1.0, 0.0).astype(BF16)
        rhs = jnp.concatenate(
            [jnp.concatenate([jnp.where(first_head, vw, zero), jnp.broadcast_to(ind, vw.shape)], axis=1),
             jnp.concatenate([jnp.where(first_head, zero, vw), jnp.broadcast_to(1 - ind, vw.shape)], axis=1)],
            axis=0)
        res = _dot(jnp.concatenate(probs, axis=1), rhs)
        return jnp.where(first_head, tops[0], tops[1]), res[:, LANES:], res[:, :LANES]

    def merge(m_old, l_old, o_old, m_blk, l_blk, o_blk):
        m_new = jnp.maximum(m_old, m_blk)
        w_old = jnp.exp(m_old - m_new)
        w_blk = jnp.exp(m_blk - m_new)
        return m_new, w_old * l_old + w_blk * l_blk, w_old * o_old + w_blk * o_blk

    def grouped(n_blocks, block):
        unroll = _tile(n_blocks, ATTN_UNROLL, align=1)

        def group(g, carry):
            for u in range(unroll):
                block(g * unroll + u, u)
            return carry

        lax.fori_loop(0, n_blocks // unroll, group, 0)

    def mid_class(r, carry):
        base = pl.multiple_of(r * len_mid, len_mid)

        def gather_mid(c, carry2):
            c0 = pl.multiple_of(c * gather_mid_rows, gather_mid_rows)
            src = pl.ds(r + mid * c0, gather_mid_rows, stride=mid)
            dst = pl.ds(c0, gather_mid_rows)
            x = q_ref[0, src, :]
            cq_ref[dst, :] = x
            qs_ref[dst, :] = x.astype(BF16)
            x = k_ref[0, src, :]
            ck_ref[dst, :] = x
            ks_ref[dst, :] = x.astype(BF16)
            x = v_ref[0, src, :]
            cv_ref[dst, :] = x
            vs_ref[dst, :] = x.astype(BF16)
            return carry2

        lax.fori_loop(0, len_mid // gather_mid_rows, gather_mid, 0)

        def mid_block(i, slot):
            q0, k0, kind = window(i, len_mid)
            m, l, o = partial_softmax(qs_ref[pl.ds(q0, Q_BLOCK), :], ks_ref[pl.ds(k0, K_WINDOW), :],
                                      vs_ref[pl.ds(k0, K_WINDOW), :], 1, kind)
            rows = pl.ds(base + q0, Q_BLOCK)
            m_ref[rows, :] = m
            l_ref[rows, :] = l
            acc_ref[rows, :] = o

        grouped(len_mid // Q_BLOCK, mid_block)

        for c in range(sub):
            def gather_far(g, carry2, c=c):
                g0 = pl.multiple_of(g * gather_far_rows, gather_far_rows)
                src = pl.ds(c + sub * g0, gather_far_rows, stride=sub)
                dst = pl.ds(c * len_far + g0, gather_far_rows)
                qs_ref[dst, :] = cq_ref[src, :].astype(BF16)
                ks_ref[dst, :] = ck_ref[src, :].astype(BF16)
                vs_ref[dst, :] = cv_ref[src, :].astype(BF16)
                return carry2

            lax.fori_loop(0, len_far // gather_far_rows, gather_far, 0)

        far_blocks = len_far // Q_BLOCK

        def far_block(j, slot):
            c = j // far_blocks
            q0, k0, kind = window(j - c * far_blocks, len_far)
            off = pl.multiple_of(c * len_far, len_far)
            m, l, o = partial_softmax(qs_ref[pl.ds(off + q0, Q_BLOCK), :], ks_ref[pl.ds(off + k0, K_WINDOW), :],
                                      vs_ref[pl.ds(off + k0, K_WINDOW), :], 2, kind)
            rows = pl.ds(base + c + sub * q0, Q_BLOCK, stride=sub)
            m, l, o = merge(m_ref[rows, :], l_ref[rows, :], acc_ref[rows, :], m, l, o)
            m_ref[rows, :] = m
            l_ref[rows, :] = l
            acc_ref[rows, :] = o

        grouped(sub * far_blocks, far_block)
        return carry

    lax.fori_loop(0, mid, mid_class, 0)

    per_class = Q_BLOCK // mid

    def near_block(i, slot):
        q0, k0, kind = window(i, seq)
        m, l, o = partial_softmax(q_ref[0, pl.ds(q0, Q_BLOCK), :].astype(BF16),
                                  k_ref[0, pl.ds(k0, K_WINDOW), :].astype(BF16),
                                  v_ref[0, pl.ds(k0, K_WINDOW), :].astype(BF16), 0, kind)
        j0 = pl.multiple_of(i * per_class, per_class)
        for c in range(mid):
            src = pl.ds(c * len_mid + j0, per_class)
            dst = pl.ds(c, per_class, stride=mid)
            stage_ref[slot, 0, dst, :] = m_ref[src, :]
            stage_ref[slot, 1, dst, :] = l_ref[src, :]
            stage_ref[slot, 2, dst, :] = acc_ref[src, :]
        m, l, o = merge(stage_ref[slot, 0], stage_ref[slot, 1], stage_ref[slot, 2], m, l, o)
        o_ref[0, pl.ds(q0, Q_BLOCK), :] = (o / l).astype(o_ref.dtype)

    grouped(seq // Q_BLOCK, near_block)


def _attention(qkv, bias, attn_width):
    bsz, seq, _ = qkv.shape
    n_pairs = attn_width // LANES
    (_, near), (_, mid), (_, far) = DILATED_BRANCHES
    assert near == 1 and far % mid == 0 and Q_BLOCK % mid == 0
    assert seq % (far * Q_BLOCK) == 0 and seq // far >= K_WINDOW

    def slab(col0):
        return pl.BlockSpec((1, seq, LANES), lambda p, b: (b, 0, col0 + p))

    bias_block = (1,) + bias.shape[1:]
    mid_rows = (seq // mid, LANES)
    stage = (ATTN_UNROLL, 3, Q_BLOCK, LANES)
    blocks = 3 * _nbytes((seq, LANES), F32) + _nbytes(bias_block, F32) + _nbytes((seq, LANES), BF16)
    scratch = (3 * _nbytes(mid_rows, F32) + 3 * _nbytes(mid_rows, BF16) + 3 * _nbytes((seq, LANES), F32)
               + _nbytes(stage, F32))
    return pl.pallas_call(
        functools.partial(_attn_kernel, seq=seq),
        grid=(n_pairs, bsz),
        in_specs=[slab(0), slab(n_pairs), slab(2 * n_pairs),
                  pl.BlockSpec(bias_block, lambda p, b: (p, 0, 0, 0, 0, 0))],
        out_specs=pl.BlockSpec((1, seq, LANES), lambda p, b: (b, 0, p)),
        out_shape=jax.ShapeDtypeStruct((bsz, seq, attn_width), BF16),
        scratch_shapes=([pltpu.VMEM(mid_rows, F32)] * 3 + [pltpu.VMEM(mid_rows, BF16)] * 3
                        + [pltpu.VMEM((seq, LANES), F32)] * 3 + [pltpu.VMEM(stage, F32)]),
        compiler_params=_params(("parallel", "parallel"),
                                _vmem_limit(blocks, scratch=scratch, temps=16 * _nbytes((Q_BLOCK, K_WINDOW), F32))),
        name="attention",
    )(qkv, qkv, qkv, bias)


OUT_SUBTILES = 2


def _out_kernel(at_ref, uc_ref, x_ref, w_ref, ag_ref, gate_ref, lg_ref, lb_ref, sc_ref, sh_ref, x_out, h_out,
                *, alpha):
    sub = at_ref.shape[1] // OUT_SUBTILES
    for r0 in range(0, at_ref.shape[1], sub):
        rows = pl.ds(r0, sub)
        attn = _rms_norm(at_ref[0, rows, :].astype(F32), ag_ref[...]).astype(BF16)
        mix = _dot(jnp.concatenate([attn, uc_ref[0, rows, :]], axis=-1), w_ref[...])
        x = _layer_norm(alpha * x_ref[0, rows, :] + (1.0 + gate_ref[0]) * mix, lg_ref[...], lb_ref[...])
        x_out[0, rows, :] = x
        h_out[0, rows, :] = (x * (1.0 + sc_ref[0]) + sh_ref[0]).astype(BF16)


def _out_proj(attn, uc, x, w_out, layer, attn_g, gate, ln_g, ln_b, sc, sh, alpha):
    bsz, s, d = x.shape
    aw = attn.shape[-1]
    cw = uc.shape[-1]
    tm = _tile(s, 512)

    def rows(width):
        return pl.BlockSpec((1, tm, width), lambda b, i: (b, i, 0))

    def vec(width):
        return pl.BlockSpec((1, width), lambda b, i: (0, 0))

    bvec = pl.BlockSpec((1, 1, d), lambda b, i: (b, 0, 0))
    blocks = (_nbytes((tm, aw), BF16) + _nbytes((tm, cw), BF16) + 2 * _nbytes((tm, d), F32)
              + _nbytes((tm, d), BF16) + _nbytes((d, d), BF16))
    return pl.pallas_call(
        functools.partial(_out_kernel, alpha=alpha),
        grid=(bsz, s // tm),
        in_specs=[rows(aw), rows(cw), rows(d), pl.BlockSpec((None, d, d), lambda b, i: (layer, 0, 0)), vec(aw), bvec,
                  vec(d), vec(d), bvec, bvec],
        out_specs=[rows(d), rows(d)],
        out_shape=[jax.ShapeDtypeStruct((bsz, s, d), F32), jax.ShapeDtypeStruct((bsz, s, d), BF16)],
        compiler_params=_params(("parallel", "parallel"), _vmem_limit(blocks, temps=4 * _nbytes((tm, d), F32))),
        name="out_proj",
    )(attn, uc, x, w_out, attn_g.reshape(1, aw), gate, ln_g.reshape(1, d), ln_b.reshape(1, d), sc, sh)


def _mlp_kernel(h_ref, w1_ref, b1_ref, w2_ref, b2_ref, x_ref, gate_ref, lg_ref, lb_ref, *rest, alpha, modulate):
    if modulate:
        sc_ref, sh_ref, x_out, h_out, acc_ref = rest
    else:
        x_out, acc_ref = rest
    t, f = pl.program_id(0), pl.program_id(1)
    n_tiles = pl.num_programs(0) - 1
    slot = t % 2

    def chunk():
        hid = jnp.maximum(_dot(h_ref[0], w1_ref[...]) + b1_ref[...], 0.0)
        return _dot((hid * hid).astype(BF16), w2_ref[...])

    def epilogue(src):
        ff = acc_ref[src] + b2_ref[...]
        x = _layer_norm(alpha * x_ref[0] + (1.0 + gate_ref[0]) * ff, lg_ref[...], lb_ref[...])
        x_out[0] = x
        if modulate:
            h_out[0] = (x * (1.0 + sc_ref[0]) + sh_ref[0]).astype(BF16)

    @pl.when((t == 0) & (f == 0))
    def _():
        acc_ref[1] = jnp.zeros(acc_ref.shape[1:], F32)

    for cur in range(2):
        mine = slot == cur

        @pl.when(mine & (f == 0) & (t < n_tiles))
        def _(cur=cur):
            acc_ref[cur] = chunk()
            epilogue(1 - cur)

        @pl.when(mine & (f > 0) & (t < n_tiles))
        def _(cur=cur):
            acc_ref[cur] += chunk()

        @pl.when(mine & (f == 0) & (t == n_tiles))
        def _(cur=cur):
            epilogue(1 - cur)


def _mlp(h, w1, b1, w2, b2, layer, x, gate, ln_g, ln_b, alpha, next_mod=None):
    bsz, s, d = x.shape
    ff_dim = w1.shape[-1]
    tm, tf = _tile(s, 512), _tile(ff_dim, 1024)
    per_batch = s // tm
    n_tiles, n_chunks = bsz * per_batch, ff_dim // tf
    modulate = next_mod is not None

    def prev(t):
        return jnp.maximum(t - 1, 0)

    cur_rows = pl.BlockSpec((1, tm, d), lambda t, f: (jnp.minimum(t, n_tiles - 1), 0, 0))
    prev_rows = pl.BlockSpec((1, tm, d), lambda t, f: (prev(t), 0, 0))
    prev_bvec = pl.BlockSpec((1, 1, d), lambda t, f: (prev(t) // per_batch, 0, 0))
    vec_d = pl.BlockSpec((1, d), lambda t, f: (0, 0))

    def chunk_of(t, f):
        return jnp.where(t == n_tiles, n_chunks - 1, f)

    in_specs = [
        cur_rows,
        pl.BlockSpec((None, d, tf), lambda t, f: (layer, 0, chunk_of(t, f))),
        pl.BlockSpec((1, tf), lambda t, f: (0, chunk_of(t, f))),
        pl.BlockSpec((None, tf, d), lambda t, f: (layer, chunk_of(t, f), 0)),
        vec_d, prev_rows, prev_bvec, vec_d, vec_d,
    ]
    tiles = (n_tiles, tm, d)
    args = [h.reshape(tiles), w1, b1.reshape(1, ff_dim), w2, b2.reshape(1, d), x.reshape(tiles), gate,
            ln_g.reshape(1, d), ln_b.reshape(1, d)]
    out_specs = [prev_rows]
    out_shape = [jax.ShapeDtypeStruct(tiles, F32)]
    blocks = (_nbytes((tm, d), BF16) + 2 * _nbytes((d, tf), BF16) + 2 * _nbytes((tm, d), F32))
    if modulate:
        in_specs += [prev_bvec, prev_bvec]
        args += list(next_mod)
        out_specs.append(prev_rows)
        out_shape.append(jax.ShapeDtypeStruct(tiles, BF16))
        blocks += _nbytes((tm, d), BF16)
    acc = (2, tm, d)
    out = pl.pallas_call(
        functools.partial(_mlp_kernel, alpha=alpha, modulate=modulate),
        grid=(n_tiles + 1, n_chunks),
        in_specs=in_specs,
        out_specs=out_specs,
        out_shape=out_shape,
        scratch_shapes=[pltpu.VMEM(acc, F32)],
        compiler_params=_params(("arbitrary", "arbitrary"),
                                _vmem_limit(blocks, scratch=_nbytes(acc, F32), temps=3 * _nbytes((tm, tf), F32))),
        name="mlp",
    )(*args)
    out = [o.reshape(bsz, s, -1) for o in out]
    return out if modulate else (out[0], None)


def _trunk(x, mods, emb_ln_g, emb_ln_b, w_in, conv_w, conv_b, conv_ln_g, conv_ln_b, attn_out_g, conv_out_g,
           w_out, ln1_g, ln1_b, w_mlp1, b_mlp1, w_mlp2, b_mlp2, ln2_g, ln2_b, bias):
    bsz, s, d = x.shape
    depth = w_in.shape[0]
    aw = d // 2
    cw = d - aw
    alpha = (2 * depth) ** 0.25

    def mod(layer, k):
        return mods[layer, :, k * d:(k + 1) * d].reshape(bsz, 1, d)

    x, h = _embed(x, emb_ln_g, emb_ln_b, mod(0, 1), mod(0, 0))
    for layer in range(depth):
        h2d = h.reshape(bsz * s, d)
        u = _glu_proj(h2d, w_in, layer, 3 * aw, 3 * aw + cw, cw).reshape(bsz, s, cw)
        qkv, y = _qkv_conv_proj(h, w_in, layer, 3 * aw, u, conv_w[layer])
        attn = _attention(qkv, bias, aw)
        uc = _conv_norm(y, conv_b[layer], conv_ln_g[layer], conv_ln_b[layer], conv_out_g[layer])
        x, h = _out_proj(attn, uc, x, w_out, layer, attn_out_g[layer], mod(layer, 2), ln1_g[layer],
                         ln1_b[layer], mod(layer, 4), mod(layer, 3), alpha)
        next_mod = (mod(layer + 1, 1), mod(layer + 1, 0)) if layer + 1 < depth else None
        x, h = _mlp(h, w_mlp1, b_mlp1[layer], w_mlp2, b_mlp2[layer], layer, x, mod(layer, 5),
                    ln2_g[layer], ln2_b[layer], alpha, next_mod)
    return x


def kernel(x_prompt, x_sample, c_prompt, c_sample, emb_ln_g, emb_ln_b, w_ada, b_ada, w_in, conv_w, conv_b,
           conv_ln_g, conv_ln_b, attn_out_g, conv_out_g, w_out, ln1_g, ln1_b, w_mlp1, b_mlp1, w_mlp2, b_mlp2,
           ln2_g, ln2_b):
    d = x_prompt.shape[-1]
    n_prompt, n_sample = c_prompt.shape[0], c_sample.shape[0]
    pad = (-(n_prompt + n_sample)) % 8
    c_all = jnp.concatenate([c_prompt, c_sample, jnp.zeros((pad, d), F32)], axis=0)
    mods = _ada_mod(c_all, w_ada, b_ada)
    n_heads = (d // 2) // HEAD_DIM
    bias = _bias_tiles(n_heads)
    q_scale = jnp.where(jnp.arange(w_in.shape[-1]) < d // 2, HEAD_DIM ** -0.5, 1.0).astype(F32)
    weights = (emb_ln_g, emb_ln_b, (w_in * q_scale).astype(BF16), conv_w, conv_b, conv_ln_g, conv_ln_b, attn_out_g,
               conv_out_g, w_out.astype(BF16), ln1_g, ln1_b, w_mlp1.astype(BF16), b_mlp1, w_mlp2.astype(BF16),
               b_mlp2, ln2_g, ln2_b, bias)
    y_prompt = _trunk(x_prompt, mods[:, :n_prompt], *weights)
    y_sample = _trunk(x_sample, mods[:, n_prompt:n_prompt + n_sample], *weights)
    return (y_prompt, y_sample)
```

```python
import functools

import jax
import jax.numpy as jnp
from jax import lax
from jax.experimental import pallas as pl
from jax.experimental.pallas import tpu as pltpu

F32 = jnp.float32
BF16 = jnp.bfloat16

HEAD_DIM = 64
DILATED_BRANCHES = ((128, 1), (512, 4), (2048, 16))
HALF_WINDOW = 64
CONV_KERNEL = 31
CONV_PAD = (CONV_KERNEL - 1) // 2
LN_EPS = 1e-5
NEG_INF = -1e30

LANES = 128
SUBLANES = 8
VMEM_BYTES_V7X = 64 * 1024 * 1024
VMEM_CAP = VMEM_BYTES_V7X - 8 * 1024 * 1024

Q_BLOCK = 128
K_WINDOW = Q_BLOCK + 2 * HALF_WINDOW
ATTN_UNROLL = 16
ATTN_GATHER_ROWS = 256


def _nbytes(shape, dtype):
    n = 1
    for s in shape:
        n *= s
    return n * jnp.dtype(dtype).itemsize


def _vmem_limit(pipelined, scratch=0, temps=0):
    return int(min(VMEM_CAP, 2 * pipelined + scratch + temps + (2 << 20)))


def _tile(n, preferred, align=LANES):
    t = min(preferred, n) // align * align
    while n % t:
        t -= align
    return t


def _params(semantics, vmem):
    return pltpu.CompilerParams(dimension_semantics=semantics, vmem_limit_bytes=vmem)


def _layer_norm(y, g, b):
    mu = jnp.mean(y, axis=-1, keepdims=True)
    yc = y - mu
    var = jnp.mean(yc * yc, axis=-1, keepdims=True)
    return yc * lax.rsqrt(var + LN_EPS) * g + b


def _rms_norm(y, g):
    return y * lax.rsqrt(jnp.mean(y * y, axis=-1, keepdims=True) + LN_EPS) * g


def _dot(a, b):
    return jnp.dot(a, b, preferred_element_type=F32)


def _ada_kernel(c_ref, w_ref, b_ref, o_ref):
    c = c_ref[...]
    a = c * jax.nn.sigmoid(c)
    w = w_ref[0]
    a_hi = a.astype(BF16)
    a_lo = (a - a_hi.astype(F32)).astype(BF16)
    w_hi = w.astype(BF16)
    w_lo = (w - w_hi.astype(F32)).astype(BF16)
    o_ref[0] = _dot(a_hi, w_hi) + _dot(a_hi, w_lo) + _dot(a_lo, w_hi) + b_ref[0]


def _ada_mod(c, w_ada, b_ada):
    depth, d, n = w_ada.shape
    rows = c.shape[0]
    tn = _tile(n, 1024)
    blocks = _nbytes((rows, d), F32) + _nbytes((d, tn), F32) + 2 * _nbytes((8, tn), F32)
    return pl.pallas_call(
        _ada_kernel,
        grid=(depth, n // tn),
        in_specs=[
            pl.BlockSpec((rows, d), lambda l, j: (0, 0)),
            pl.BlockSpec((1, d, tn), lambda l, j: (l, 0, j)),
            pl.BlockSpec((1, 1, tn), lambda l, j: (l, 0, j)),
        ],
        out_specs=pl.BlockSpec((1, rows, tn), lambda l, j: (l, 0, j)),
        out_shape=jax.ShapeDtypeStruct((depth, rows, n), F32),
        compiler_params=_params(("parallel", "parallel"), _vmem_limit(blocks, temps=2 * _nbytes((d, tn), F32))),
        name="ada_mod",
    )(c, w_ada, b_ada.reshape(depth, 1, n))


def _embed_kernel(x_ref, g_ref, b_ref, sc_ref, sh_ref, x_out, h_out):
    x = _layer_norm(x_ref[0], g_ref[...], b_ref[...])
    x_out[0] = x
    h_out[0] = (x * (1.0 + sc_ref[0]) + sh_ref[0]).astype(BF16)


def _embed(x, g, b, sc, sh):
    bsz, s, d = x.shape
    ts = _tile(s, 512)
    row = pl.BlockSpec((1, ts, d), lambda i, j: (i, j, 0))
    vec = pl.BlockSpec((1, d), lambda i, j: (0, 0))
    bvec = pl.BlockSpec((1, 1, d), lambda i, j: (i, 0, 0))
    blocks = 2 * _nbytes((ts, d), F32) + _nbytes((ts, d), BF16)
    return pl.pallas_call(
        _embed_kernel,
        grid=(bsz, s // ts),
        in_specs=[row, vec, vec, bvec, bvec],
        out_specs=[row, row],
        out_shape=[jax.ShapeDtypeStruct((bsz, s, d), F32), jax.ShapeDtypeStruct((bsz, s, d), BF16)],
        compiler_params=_params(("parallel", "parallel"), _vmem_limit(blocks, temps=4 * _nbytes((ts, d), F32))),
        name="embed_ln",
    )(x, g.reshape(1, d), b.reshape(1, d), sc, sh)


CONV_HALO = 16
CONV_ROWS = 16
CONV_LANES = 256


def _qkv_conv_kernel(h_ref, w_ref, prev_ref, cur_ref, next_ref, taps_ref, o_ref, y_ref, sh_ref, *, tiles_per_seq):
    o_ref[...] = _dot(h_ref[...], w_ref[...])

    j = pl.program_id(1) % tiles_per_seq
    rows, lanes = cur_ref.shape[1], cur_ref.shape[2]
    zeros = jnp.zeros((CONV_HALO, lanes), F32)
    sh_ref[0, 0:CONV_HALO, :] = jnp.where(j > 0, prev_ref[0], zeros)
    sh_ref[0, CONV_HALO:CONV_HALO + rows, :] = cur_ref[0]
    sh_ref[0, CONV_HALO + rows:, :] = jnp.where(j < tiles_per_seq - 1, next_ref[0], zeros)
    shifted_rows = rows + 2 * CONV_HALO - SUBLANES
    for b in range(1, SUBLANES):
        sh_ref[b, 0:shifted_rows, :] = sh_ref[0, b:b + shifted_rows, :]

    slab_rows = CONV_ROWS + 2 * CONV_HALO - SUBLANES
    acc = None
    for r0 in range(0, rows, CONV_ROWS):
        if acc is None:
            acc = jnp.zeros((CONV_ROWS // SUBLANES, SUBLANES, lanes), F32)
        else:
            bits = lax.bitcast_convert_type(acc, jnp.uint32)
            acc = lax.shift_right_logical(lax.shift_right_logical(bits, jnp.uint32(16)), jnp.uint32(16)).astype(F32)
        for shift in range(SUBLANES):
            slab = sh_ref[shift, r0:r0 + slab_rows, :].reshape(slab_rows // SUBLANES, SUBLANES, lanes)
            for tap in range(CONV_KERNEL):
                offset = tap + (CONV_HALO - CONV_PAD)
                if offset % SUBLANES == shift:
                    first = offset // SUBLANES
                    acc = acc + slab[first:first + CONV_ROWS // SUBLANES] * taps_ref[tap][None]
        y_ref[0, r0:r0 + CONV_ROWS, :] = acc.reshape(CONV_ROWS, lanes)


def _qkv_conv_proj(h, w_in, layer, n_cols, u, conv_w):
    bsz, s, d = h.shape
    width = u.shape[-1]
    m = bsz * s
    tm = _tile(s, 1024)
    n_groups = width // CONV_LANES
    tn = n_cols // n_groups
    assert tn % LANES == 0 and width % CONV_LANES == 0
    tiles_per_seq = s // tm
    halo_blocks = tm // CONV_HALO
    last_halo = s // CONV_HALO - 1

    def seq_tile(i):
        return i // tiles_per_seq, i % tiles_per_seq

    def prev_map(n, i):
        b, j = seq_tile(i)
        return (b, jnp.maximum(j * halo_blocks - 1, 0), n)

    def cur_map(n, i):
        b, j = seq_tile(i)
        return (b, j, n)

    def next_map(n, i):
        b, j = seq_tile(i)
        return (b, jnp.minimum((j + 1) * halo_blocks, last_halo), n)

    taps = jnp.broadcast_to(conv_w[:, None, :], (CONV_KERNEL, SUBLANES, width))
    shifted = (SUBLANES, tm + 2 * CONV_HALO, CONV_LANES)
    blocks = (_nbytes((tm, d), BF16) + _nbytes((d, tn), BF16) + _nbytes((tm, tn), F32)
              + 2 * _nbytes((CONV_HALO, CONV_LANES), F32) + 2 * _nbytes((tm, CONV_LANES), F32)
              + _nbytes((CONV_KERNEL, SUBLANES, CONV_LANES), F32))
    qkv, y = pl.pallas_call(
        functools.partial(_qkv_conv_kernel, tiles_per_seq=tiles_per_seq),
        grid=(n_groups, m // tm),
        in_specs=[
            pl.BlockSpec((tm, d), lambda n, i: (i, 0)),
            pl.BlockSpec((None, d, tn), lambda n, i: (layer, 0, n)),
            pl.BlockSpec((1, CONV_HALO, CONV_LANES), prev_map),
            pl.BlockSpec((1, tm, CONV_LANES), cur_map),
            pl.BlockSpec((1, CONV_HALO, CONV_LANES), next_map),
            pl.BlockSpec((CONV_KERNEL, SUBLANES, CONV_LANES), lambda n, i: (0, 0, n)),
        ],
        out_specs=[pl.BlockSpec((tm, tn), lambda n, i: (i, n)),
                   pl.BlockSpec((1, tm, CONV_LANES), cur_map)],
        out_shape=[jax.ShapeDtypeStruct((m, n_cols), F32), jax.ShapeDtypeStruct((bsz, s, width), F32)],
        scratch_shapes=[pltpu.VMEM(shifted, F32)],
        compiler_params=_params(("parallel", "parallel"),
                                _vmem_limit(blocks, scratch=_nbytes(shifted, F32), temps=2 * _nbytes((tm, tn), F32))),
        name="qkv_conv_proj",
    )(h.reshape(m, d), w_in, u, u, u, taps)
    return qkv.reshape(bsz, s, n_cols), y


def _conv_norm_kernel(y_ref, cb_ref, lg_ref, lb_ref, og_ref, o_ref):
    y = _layer_norm(y_ref[0] + cb_ref[...], lg_ref[...], lb_ref[...])
    y = y * jax.nn.sigmoid(y)
    o_ref[0] = _rms_norm(y, og_ref[...]).astype(o_ref.dtype)


def _conv_norm(y, conv_b, ln_g, ln_b, out_g):
    bsz, s, width = y.shape
    ts = _tile(s, 512)
    row = pl.BlockSpec((1, ts, width), lambda b, j: (b, j, 0))
    vec = pl.BlockSpec((1, width), lambda b, j: (0, 0))
    blocks = _nbytes((ts, width), F32) + _nbytes((ts, width), BF16)
    return pl.pallas_call(
        _conv_norm_kernel,
        grid=(bsz, s // ts),
        in_specs=[row, vec, vec, vec, vec],
        out_specs=row,
        out_shape=jax.ShapeDtypeStruct((bsz, s, width), BF16),
        compiler_params=_params(("parallel", "parallel"), _vmem_limit(blocks, temps=4 * _nbytes((ts, width), F32))),
        name="conv_norm",
    )(y, conv_b.reshape(1, width), ln_g.reshape(1, width), ln_b.reshape(1, width), out_g.reshape(1, width))


def _glu_kernel(h_ref, wv_ref, wg_ref, *rest):
    n_cast = len(rest) // 2
    o_ref = rest[n_cast]
    h = h_ref[...]
    o_ref[...] = _dot(h, wv_ref[...]) * jax.nn.sigmoid(_dot(h, wg_ref[...]))
    for src, dst in zip(rest[:n_cast], rest[n_cast + 1:]):
        dst[...] = src[...].astype(dst.dtype)


def _glu_proj(h2d, w_in, layer, val_col, gate_col, width, cast=()):
    m, d = h2d.shape
    tm, tn = _tile(m, 1024), _tile(width, 512)
    v_blk, g_blk = val_col // tn, gate_col // tn
    m_tiles = m // tm
    steps = (width // tn) * m_tiles
    blocks = _nbytes((tm, d), BF16) + 2 * _nbytes((d, tn), BF16) + _nbytes((tm, tn), F32)
    in_specs = [
        pl.BlockSpec((tm, d), lambda n, i: (i, 0)),
        pl.BlockSpec((None, d, tn), lambda n, i: (layer, 0, v_blk + n)),
        pl.BlockSpec((None, d, tn), lambda n, i: (layer, 0, g_blk + n)),
    ]
    out_specs = [pl.BlockSpec((tm, tn), lambda n, i: (i, n))]
    out_shape = [jax.ShapeDtypeStruct((m, width), F32)]
    for w in cast:
        depth, rows, cols = w.shape
        per_layer = steps // depth
        chunk = rows // per_layer
        assert steps % depth == 0 and rows % per_layer == 0 and chunk % 16 == 0

        def chunk_map(n, i, per_layer=per_layer):
            step = n * m_tiles + i
            return (step // per_layer, step % per_layer, 0)

        in_specs.append(pl.BlockSpec((None, chunk, cols), chunk_map))
        out_specs.append(pl.BlockSpec((None, chunk, cols), chunk_map))
        out_shape.append(jax.ShapeDtypeStruct(w.shape, BF16))
        blocks += _nbytes((chunk, cols), F32) + _nbytes((chunk, cols), BF16)
    return pl.pallas_call(
        _glu_kernel,
        grid=(width // tn, m_tiles),
        in_specs=in_specs,
        out_specs=out_specs,
        out_shape=out_shape,
        compiler_params=_params(("parallel", "parallel"), _vmem_limit(blocks, temps=4 * _nbytes((tm, tn), F32))),
        name="glu_proj",
    )(h2d, w_in, w_in, *cast)


def _bias_tiles(n_heads):
    slopes = 2.0 ** (-8.0 * jnp.arange(1, n_heads + 1, dtype=F32) / n_heads)
    q = jnp.arange(Q_BLOCK)[:, None]
    c = jnp.arange(K_WINDOW)[None, :]
    branches = []
    for window, dilation in DILATED_BRANCHES:
        assert window == 2 * dilation * HALF_WINDOW
        tiles = []
        for key_start_minus_q_start in (0, -HALF_WINDOW, -2 * HALF_WINDOW):
            rel = key_start_minus_q_start + c - q
            dist = (dilation * jnp.abs(rel)).astype(F32)
            bias = -slopes[:, None, None] * dist[None]
            tiles.append(jnp.where((jnp.abs(rel) <= HALF_WINDOW)[None], bias, NEG_INF))
        branches.append(jnp.stack(tiles, axis=0))
    b = jnp.stack(branches, axis=0)
    b = b.reshape(len(DILATED_BRANCHES), 3, n_heads // 2, 2, Q_BLOCK, K_WINDOW)
    return b.transpose(2, 0, 1, 3, 4, 5)


def _attn_kernel(q_ref, k_ref, v_ref, bias_ref, o_ref, cq_ref, ck_ref, cv_ref, qs_ref, ks_ref, vs_ref,
                 acc_ref, m_ref, l_ref, stage_ref, *, seq):
    lane = lax.broadcasted_iota(jnp.int32, (1, LANES), 1)
    first_head = lane < HEAD_DIM
    _, (_, mid), (_, far) = DILATED_BRANCHES
    sub = far // mid
    len_mid, len_far = seq // mid, seq // far
    gather_mid_rows = min(ATTN_GATHER_ROWS, len_mid)
    gather_far_rows = min(ATTN_GATHER_ROWS, len_far)

    def window(i, length):
        q0 = pl.multiple_of(i * Q_BLOCK, Q_BLOCK)
        k0 = pl.multiple_of(jnp.clip(q0 - HALF_WINDOW, 0, length - K_WINDOW), HALF_WINDOW)
        return q0, k0, (q0 - k0) // HALF_WINDOW

    def partial_softmax(q, kw, vw, branch, kind):
        tops, probs = [], []
        for head in range(2):
            mine = first_head if head == 0 else jnp.logical_not(first_head)
            qh = jnp.where(mine, q, jnp.zeros_like(q))
            s = lax.dot_general(qh, kw, (((1,), (1,)), ((), ())), preferred_element_type=F32)
            s = s + bias_ref[0, branch, kind, head]
            m = jnp.max(s, axis=-1, keepdims=True)
            tops.append(m)
            probs.append(jnp.exp(s - m).astype(BF16))
        zero = jnp.zeros_like(vw)
        ind = jnp.where(first_head, 1.0, 0.0).astype(BF16)
        rhs = jnp.concatenate(
            [jnp.concatenate([jnp.where(first_head, vw, zero), jnp.broadcast_to(ind, vw.shape)], axis=1),
             jnp.concatenate([jnp.where(first_head, zero, vw), jnp.broadcast_to(1 - ind, vw.shape)], axis=1)],
            axis=0)
        res = _dot(jnp.concatenate(probs, axis=1), rhs)
        return jnp.where(first_head, tops[0], tops[1]), res[:, LANES:], res[:, :LANES]

    def merge(m_old, l_old, o_old, m_blk, l_blk, o_blk):
        m_new = jnp.maximum(m_old, m_blk)
        w_old = jnp.exp(m_old - m_new)
        w_blk = jnp.exp(m_blk - m_new)
        return m_new, w_old * l_old + w_blk * l_blk, w_old * o_old + w_blk * o_blk

    def grouped(n_blocks, block):
        unroll = _tile(n_blocks, ATTN_UNROLL, align=1)

        def group(g, carry):
            for u in range(unroll):
                block(g * unroll + u, u)
            return carry

        lax.fori_loop(0, n_blocks // unroll, group, 0)

    def mid_class(r, carry):
        base = pl.multiple_of(r * len_mid, len_mid)

        def gather_mid(c, carry2):
            c0 = pl.multiple_of(c * gather_mid_rows, gather_mid_rows)
            src = pl.ds(r + mid * c0, gather_mid_rows, stride=mid)
            dst = pl.ds(c0, gather_mid_rows)
            x = q_ref[0, src, :]
            cq_ref[dst, :] = x
            qs_ref[dst, :] = x.astype(BF16)
            x = k_ref[0, src, :]
            ck_ref[dst, :] = x
            ks_ref[dst, :] = x.astype(BF16)
            x = v_ref[0, src, :]
            cv_ref[dst, :] = x
            vs_ref[dst, :] = x.astype(BF16)
            return carry2

        lax.fori_loop(0, len_mid // gather_mid_rows, gather_mid, 0)

        def mid_block(i, slot):
            q0, k0, kind = window(i, len_mid)
            m, l, o = partial_softmax(qs_ref[pl.ds(q0, Q_BLOCK), :], ks_ref[pl.ds(k0, K_WINDOW), :],
                                      vs_ref[pl.ds(k0, K_WINDOW), :], 1, kind)
            rows = pl.ds(base + q0, Q_BLOCK)
            m_ref[rows, :] = m
            l_ref[rows, :] = l
            acc_ref[rows, :] = o

        grouped(len_mid // Q_BLOCK, mid_block)

        for c in range(sub):
            def gather_far(g, carry2, c=c):
                g0 = pl.multiple_of(g * gather_far_rows, gather_far_rows)
                src = pl.ds(c + sub * g0, gather_far_rows, stride=sub)
                dst = pl.ds(c * len_far + g0, gather_far_rows)
                qs_ref[dst, :] = cq_ref[src, :].astype(BF16)
                ks_ref[dst, :] = ck_ref[src, :].astype(BF16)
                vs_ref[dst, :] = cv_ref[src, :].astype(BF16)
                return carry2

            lax.fori_loop(0, len_far // gather_far_rows, gather_far, 0)

        far_blocks = len_far // Q_BLOCK

        def far_block(j, slot):
            c = j // far_blocks
            q0, k0, kind = window(j - c * far_blocks, len_far)
            off = pl.multiple_of(c * len_far, len_far)
            m, l, o = partial_softmax(qs_ref[pl.ds(off + q0, Q_BLOCK), :], ks_ref[pl.ds(off + k0, K_WINDOW), :],
                                      vs_ref[pl.ds(off + k0, K_WINDOW), :], 2, kind)
            rows = pl.ds(base + c + sub * q0, Q_BLOCK, stride=sub)
            m, l, o = merge(m_ref[rows, :], l_ref[rows, :], acc_ref[rows, :], m, l, o)
            m_ref[rows, :] = m
            l_ref[rows, :] = l
            acc_ref[rows, :] = o

        grouped(sub * far_blocks, far_block)
        return carry

    lax.fori_loop(0, mid, mid_class, 0)

    per_class = Q_BLOCK // mid

    def near_block(i, slot):
        q0, k0, kind = window(i, seq)
        m, l, o = partial_softmax(q_ref[0, pl.ds(q0, Q_BLOCK), :].astype(BF16),
                                  k_ref[0, pl.ds(k0, K_WINDOW), :].astype(BF16),
                                  v_ref[0, pl.ds(k0, K_WINDOW), :].astype(BF16), 0, kind)
        j0 = pl.multiple_of(i * per_class, per_class)
        for c in range(mid):
            src = pl.ds(c * len_mid + j0, per_class)
            dst = pl.ds(c, per_class, stride=mid)
            stage_ref[slot, 0, dst, :] = m_ref[src, :]
            stage_ref[slot, 1, dst, :] = l_ref[src, :]
            stage_ref[slot, 2, dst, :] = acc_ref[src, :]
        m, l, o = merge(stage_ref[slot, 0], stage_ref[slot, 1], stage_ref[slot, 2], m, l, o)
        o_ref[0, pl.ds(q0, Q_BLOCK), :] = (o / l).astype(o_ref.dtype)

    grouped(seq // Q_BLOCK, near_block)


def _attention(qkv, bias, attn_width):
    bsz, seq, _ = qkv.shape
    n_pairs = attn_width // LANES
    (_, near), (_, mid), (_, far) = DILATED_BRANCHES
    assert near == 1 and far % mid == 0 and Q_BLOCK % mid == 0
    assert seq % (far * Q_BLOCK) == 0 and seq // far >= K_WINDOW

    def slab(col0):
        return pl.BlockSpec((1, seq, LANES), lambda p, b: (b, 0, col0 + p))

    bias_block = (1,) + bias.shape[1:]
    mid_rows = (seq // mid, LANES)
    stage = (ATTN_UNROLL, 3, Q_BLOCK, LANES)
    blocks = 3 * _nbytes((seq, LANES), F32) + _nbytes(bias_block, F32) + _nbytes((seq, LANES), BF16)
    scratch = (3 * _nbytes(mid_rows, F32) + 3 * _nbytes(mid_rows, BF16) + 3 * _nbytes((seq, LANES), F32)
               + _nbytes(stage, F32))
    return pl.pallas_call(
        functools.partial(_attn_kernel, seq=seq),
        grid=(n_pairs, bsz),
        in_specs=[slab(0), slab(n_pairs), slab(2 * n_pairs),
                  pl.BlockSpec(bias_block, lambda p, b: (p, 0, 0, 0, 0, 0))],
        out_specs=pl.BlockSpec((1, seq, LANES), lambda p, b: (b, 0, p)),
        out_shape=jax.ShapeDtypeStruct((bsz, seq, attn_width), BF16),
        scratch_shapes=([pltpu.VMEM(mid_rows, F32)] * 3 + [pltpu.VMEM(mid_rows, BF16)] * 3
                        + [pltpu.VMEM((seq, LANES), F32)] * 3 + [pltpu.VMEM(stage, F32)]),
        compiler_params=_params(("parallel", "parallel"),
                                _vmem_limit(blocks, scratch=scratch, temps=16 * _nbytes((Q_BLOCK, K_WINDOW), F32))),
        name="attention",
    )(qkv, qkv, qkv, bias)


OUT_SUBTILES = 2


def _out_kernel(at_ref, uc_ref, x_ref, w_ref, ag_ref, gate_ref, lg_ref, lb_ref, sc_ref, sh_ref, x_out, h_out,
                *, alpha):
    sub = at_ref.shape[1] // OUT_SUBTILES
    for r0 in range(0, at_ref.shape[1], sub):
        rows = pl.ds(r0, sub)
        attn = _rms_norm(at_ref[0, rows, :].astype(F32), ag_ref[...]).astype(BF16)
        mix = _dot(jnp.concatenate([attn, uc_ref[0, rows, :]], axis=-1), w_ref[...])
        x = _layer_norm(alpha * x_ref[0, rows, :] + (1.0 + gate_ref[0]) * mix, lg_ref[...], lb_ref[...])
        x_out[0, rows, :] = x
        h_out[0, rows, :] = (x * (1.0 + sc_ref[0]) + sh_ref[0]).astype(BF16)


def _out_proj(attn, uc, x, w_out, layer, attn_g, gate, ln_g, ln_b, sc, sh, alpha):
    bsz, s, d = x.shape
    aw = attn.shape[-1]
    cw = uc.shape[-1]
    tm = _tile(s, 512)

    def rows(width):
        return pl.BlockSpec((1, tm, width), lambda b, i: (b, i, 0))

    def vec(width):
        return pl.BlockSpec((1, width), lambda b, i: (0, 0))

    bvec = pl.BlockSpec((1, 1, d), lambda b, i: (b, 0, 0))
    blocks = (_nbytes((tm, aw), BF16) + _nbytes((tm, cw), BF16) + 2 * _nbytes((tm, d), F32)
              + _nbytes((tm, d), BF16) + _nbytes((d, d), BF16))
    return pl.pallas_call(
        functools.partial(_out_kernel, alpha=alpha),
        grid=(bsz, s // tm),
        in_specs=[rows(aw), rows(cw), rows(d), pl.BlockSpec((None, d, d), lambda b, i: (layer, 0, 0)), vec(aw), bvec,
                  vec(d), vec(d), bvec, bvec],
        out_specs=[rows(d), rows(d)],
        out_shape=[jax.ShapeDtypeStruct((bsz, s, d), F32), jax.ShapeDtypeStruct((bsz, s, d), BF16)],
        compiler_params=_params(("parallel", "parallel"), _vmem_limit(blocks, temps=4 * _nbytes((tm, d), F32))),
        name="out_proj",
    )(attn, uc, x, w_out, attn_g.reshape(1, aw), gate, ln_g.reshape(1, d), ln_b.reshape(1, d), sc, sh)


def _mlp_kernel(h_ref, w1_ref, b1_ref, w2_ref, b2_ref, x_ref, gate_ref, lg_ref, lb_ref, *rest, alpha, modulate):
    if modulate:
        sc_ref, sh_ref, x_out, h_out, acc_ref = rest
    else:
        x_out, acc_ref = rest
    t, f = pl.program_id(0), pl.program_id(1)
    n_tiles = pl.num_programs(0) - 1
    slot = t % 2

    def chunk():
        hid = jnp.maximum(_dot(h_ref[0], w1_ref[...]) + b1_ref[...], 0.0)
        return _dot((hid * hid).astype(BF16), w2_ref[...])

    def epilogue(src):
        ff = acc_ref[src] + b2_ref[...]
        x = _layer_norm(alpha * x_ref[0] + (1.0 + gate_ref[0]) * ff, lg_ref[...], lb_ref[...])
        x_out[0] = x
        if modulate:
            h_out[0] = (x * (1.0 + sc_ref[0]) + sh_ref[0]).astype(BF16)

    @pl.when((t == 0) & (f == 0))
    def _():
        acc_ref[1] = jnp.zeros(acc_ref.shape[1:], F32)

    for cur in range(2):
        mine = slot == cur

        @pl.when(mine & (f == 0) & (t < n_tiles))
        def _(cur=cur):
            acc_ref[cur] = chunk()
            epilogue(1 - cur)

        @pl.when(mine & (f > 0) & (t < n_tiles))
        def _(cur=cur):
            acc_ref[cur] += chunk()

        @pl.when(mine & (f == 0) & (t == n_tiles))
        def _(cur=cur):
            epilogue(1 - cur)


def _mlp(h, w1, b1, w2, b2, layer, x, gate, ln_g, ln_b, alpha, next_mod=None):
    bsz, s, d = x.shape
    ff_dim = w1.shape[-1]
    tm, tf = _tile(s, 512), _tile(ff_dim, 1024)
    per_batch = s // tm
    n_tiles, n_chunks = bsz * per_batch, ff_dim // tf
    modulate = next_mod is not None

    def prev(t):
        return jnp.maximum(t - 1, 0)

    cur_rows = pl.BlockSpec((1, tm, d), lambda t, f: (jnp.minimum(t, n_tiles - 1), 0, 0))
    prev_rows = pl.BlockSpec((1, tm, d), lambda t, f: (prev(t), 0, 0))
    prev_bvec = pl.BlockSpec((1, 1, d), lambda t, f: (prev(t) // per_batch, 0, 0))
    vec_d = pl.BlockSpec((1, d), lambda t, f: (0, 0))

    def chunk_of(t, f):
        return jnp.where(t == n_tiles, n_chunks - 1, f)

    in_specs = [
        cur_rows,
        pl.BlockSpec((None, d, tf), lambda t, f: (layer, 0, chunk_of(t, f))),
        pl.BlockSpec((1, tf), lambda t, f: (0, chunk_of(t, f))),
        pl.BlockSpec((None, tf, d), lambda t, f: (layer, chunk_of(t, f), 0)),
        vec_d, prev_rows, prev_bvec, vec_d, vec_d,
    ]
    tiles = (n_tiles, tm, d)
    args = [h.reshape(tiles), w1, b1.reshape(1, ff_dim), w2, b2.reshape(1, d), x.reshape(tiles), gate,
            ln_g.reshape(1, d), ln_b.reshape(1, d)]
    out_specs = [prev_rows]
    out_shape = [jax.ShapeDtypeStruct(tiles, F32)]
    blocks = (_nbytes((tm, d), BF16) + 2 * _nbytes((d, tf), BF16) + 2 * _nbytes((tm, d), F32))
    if modulate:
        in_specs += [prev_bvec, prev_bvec]
        args += list(next_mod)
        out_specs.append(prev_rows)
        out_shape.append(jax.ShapeDtypeStruct(tiles, BF16))
        blocks += _nbytes((tm, d), BF16)
    acc = (2, tm, d)
    out = pl.pallas_call(
        functools.partial(_mlp_kernel, alpha=alpha, modulate=modulate),
        grid=(n_tiles + 1, n_chunks),
        in_specs=in_specs,
        out_specs=out_specs,
        out_shape=out_shape,
        scratch_shapes=[pltpu.VMEM(acc, F32)],
        compiler_params=_params(("arbitrary", "arbitrary"),
                                _vmem_limit(blocks, scratch=_nbytes(acc, F32), temps=3 * _nbytes((tm, tf), F32))),
        name="mlp",
    )(*args)
    out = [o.reshape(bsz, s, -1) for o in out]
    return out if modulate else (out[0], None)


def _trunk(x, mods, emb_ln_g, emb_ln_b, w_in, conv_w, conv_b, conv_ln_g, conv_ln_b, attn_out_g, conv_out_g,
           w_out, ln1_g, ln1_b, w_mlp1, b_mlp1, w_mlp2, b_mlp2, ln2_g, ln2_b, bias):
    bsz, s, d = x.shape
    depth = w_in.shape[0]
    aw = d // 2
    cw = d - aw
    alpha = (2 * depth) ** 0.25

    def mod(layer, k):
        return mods[layer, :, k * d:(k + 1) * d].reshape(bsz, 1, d)

    x, h = _embed(x, emb_ln_g, emb_ln_b, mod(0, 1), mod(0, 0))
    for layer in range(depth):
        h2d = h.reshape(bsz * s, d)
        cast = tuple(w for w in (w_mlp1, w_mlp2) if w.dtype != BF16)
        u, *converted = _glu_proj(h2d, w_in, layer, 3 * aw, 3 * aw + cw, cw, cast)
        if converted:
            w_mlp1, w_mlp2 = converted
        u = u.reshape(bsz, s, cw)
        qkv, y = _qkv_conv_proj(h, w_in, layer, 3 * aw, u, conv_w[layer])
        attn = _attention(qkv, bias, aw)
        uc = _conv_norm(y, conv_b[layer], conv_ln_g[layer], conv_ln_b[layer], conv_out_g[layer])
        x, h = _out_proj(attn, uc, x, w_out, layer, attn_out_g[layer], mod(layer, 2), ln1_g[layer],
                         ln1_b[layer], mod(layer, 4), mod(layer, 3), alpha)
        next_mod = (mod(layer + 1, 1), mod(layer + 1, 0)) if layer + 1 < depth else None
        x, h = _mlp(h, w_mlp1, b_mlp1[layer], w_mlp2, b_mlp2[layer], layer, x, mod(layer, 5),
                    ln2_g[layer], ln2_b[layer], alpha, next_mod)
    return x, w_mlp1, w_mlp2


def kernel(x_prompt, x_sample, c_prompt, c_sample, emb_ln_g, emb_ln_b, w_ada, b_ada, w_in, conv_w, conv_b,
           conv_ln_g, conv_ln_b, attn_out_g, conv_out_g, w_out, ln1_g, ln1_b, w_mlp1, b_mlp1, w_mlp2, b_mlp2,
           ln2_g, ln2_b):
    d = x_prompt.shape[-1]
    n_prompt, n_sample = c_prompt.shape[0], c_sample.shape[0]
    pad = (-(n_prompt + n_sample)) % 8
    c_all = jnp.concatenate([c_prompt, c_sample, jnp.zeros((pad, d), F32)], axis=0)
    mods = _ada_mod(c_all, w_ada, b_ada)
    n_heads = (d // 2) // HEAD_DIM
    bias = _bias_tiles(n_heads)
    q_scale = jnp.where(jnp.arange(w_in.shape[-1]) < d // 2, HEAD_DIM ** -0.5, 1.0).astype(F32)
    mixer = (emb_ln_g, emb_ln_b, (w_in * q_scale).astype(BF16), conv_w, conv_b, conv_ln_g, conv_ln_b, attn_out_g,
             conv_out_g, w_out.astype(BF16), ln1_g, ln1_b)

    def run(x, trunk_mods, w1, w2):
        return _trunk(x, trunk_mods, *mixer, w1, b_mlp1, w2, b_mlp2, ln2_g, ln2_b, bias)

    y_prompt, w1_bf16, w2_bf16 = run(x_prompt, mods[:, :n_prompt], w_mlp1, w_mlp2)
    y_sample, _, _ = run(x_sample, mods[:, n_prompt:n_prompt + n_sample], w1_bf16, w2_bf16)
    return (y_prompt, y_sample)
```

```python
import functools

import jax
import jax.numpy as jnp
from jax import lax
from jax.experimental import pallas as pl
from jax.experimental.pallas import tpu as pltpu

F32 = jnp.float32
BF16 = jnp.bfloat16

HEAD_DIM = 64
DILATED_BRANCHES = ((128, 1), (512, 4), (2048, 16))
HALF_WINDOW = 64
CONV_KERNEL = 31
CONV_PAD = (CONV_KERNEL - 1) // 2
LN_EPS = 1e-5
NEG_INF = -1e30

LANES = 128
SUBLANES = 8
VMEM_BYTES_V7X = 64 * 1024 * 1024
VMEM_CAP = VMEM_BYTES_V7X - 8 * 1024 * 1024

Q_BLOCK = 128
K_WINDOW = Q_BLOCK + 2 * HALF_WINDOW
ATTN_UNROLL = 16
ATTN_GATHER_ROWS = 256


def _nbytes(shape, dtype):
    n = 1
    for s in shape:
        n *= s
    return n * jnp.dtype(dtype).itemsize


def _vmem_limit(pipelined, scratch=0, temps=0):
    return int(min(VMEM_CAP, 2 * pipelined + scratch + temps + (2 << 20)))


def _tile(n, preferred, align=LANES):
    t = min(preferred, n) // align * align
    while n % t:
        t -= align
    return t


def _params(semantics, vmem):
    return pltpu.CompilerParams(dimension_semantics=semantics, vmem_limit_bytes=vmem)


def _layer_norm(y, g, b):
    mu = jnp.mean(y, axis=-1, keepdims=True)
    yc = y - mu
    var = jnp.mean(yc * yc, axis=-1, keepdims=True)
    return yc * lax.rsqrt(var + LN_EPS) * g + b


def _rms_norm(y, g):
    return y * lax.rsqrt(jnp.mean(y * y, axis=-1, keepdims=True) + LN_EPS) * g


def _dot(a, b):
    return jnp.dot(a, b, preferred_element_type=F32)


def _ada_kernel(c_ref, w_ref, b_ref, o_ref):
    c = c_ref[...]
    a = c * jax.nn.sigmoid(c)
    w = w_ref[0]
    a_hi = a.astype(BF16)
    a_lo = (a - a_hi.astype(F32)).astype(BF16)
    w_hi = w.astype(BF16)
    w_lo = (w - w_hi.astype(F32)).astype(BF16)
    o_ref[0] = _dot(a_hi, w_hi) + _dot(a_hi, w_lo) + _dot(a_lo, w_hi) + b_ref[0]


def _ada_mod(c, w_ada, b_ada):
    depth, d, n = w_ada.shape
    rows = c.shape[0]
    tn = _tile(n, 1024)
    blocks = _nbytes((rows, d), F32) + _nbytes((d, tn), F32) + 2 * _nbytes((8, tn), F32)
    return pl.pallas_call(
        _ada_kernel,
        grid=(depth, n // tn),
        in_specs=[
            pl.BlockSpec((rows, d), lambda l, j: (0, 0)),
            pl.BlockSpec((1, d, tn), lambda l, j: (l, 0, j)),
            pl.BlockSpec((1, 1, tn), lambda l, j: (l, 0, j)),
        ],
        out_specs=pl.BlockSpec((1, rows, tn), lambda l, j: (l, 0, j)),
        out_shape=jax.ShapeDtypeStruct((depth, rows, n), F32),
        compiler_params=_params(("parallel", "parallel"), _vmem_limit(blocks, temps=2 * _nbytes((d, tn), F32))),
        name="ada_mod",
    )(c, w_ada, b_ada.reshape(depth, 1, n))


def _embed_kernel(x_ref, g_ref, b_ref, sc_ref, sh_ref, x_out, h_out):
    x = _layer_norm(x_ref[0], g_ref[...], b_ref[...])
    x_out[0] = x
    h_out[0] = (x * (1.0 + sc_ref[0]) + sh_ref[0]).astype(BF16)


def _embed(x, g, b, sc, sh):
    bsz, s, d = x.shape
    ts = _tile(s, 512)
    row = pl.BlockSpec((1, ts, d), lambda i, j: (i, j, 0))
    vec = pl.BlockSpec((1, d), lambda i, j: (0, 0))
    bvec = pl.BlockSpec((1, 1, d), lambda i, j: (i, 0, 0))
    blocks = 2 * _nbytes((ts, d), F32) + _nbytes((ts, d), BF16)
    return pl.pallas_call(
        _embed_kernel,
        grid=(bsz, s // ts),
        in_specs=[row, vec, vec, bvec, bvec],
        out_specs=[row, row],
        out_shape=[jax.ShapeDtypeStruct((bsz, s, d), F32), jax.ShapeDtypeStruct((bsz, s, d), BF16)],
        compiler_params=_params(("parallel", "parallel"), _vmem_limit(blocks, temps=4 * _nbytes((ts, d), F32))),
        name="embed_ln",
    )(x, g.reshape(1, d), b.reshape(1, d), sc, sh)


CONV_HALO = 16
CONV_ROWS = 16
CONV_LANES = 256


def _qkv_conv_kernel(h_ref, w_ref, prev_ref, cur_ref, next_ref, taps_ref, o_ref, y_ref, sh_ref, *, tiles_per_seq):
    o_ref[...] = _dot(h_ref[...], w_ref[...])

    j = pl.program_id(1) % tiles_per_seq
    rows, lanes = cur_ref.shape[1], cur_ref.shape[2]
    zeros = jnp.zeros((CONV_HALO, lanes), F32)
    sh_ref[0, 0:CONV_HALO, :] = jnp.where(j > 0, prev_ref[0], zeros)
    sh_ref[0, CONV_HALO:CONV_HALO + rows, :] = cur_ref[0]
    sh_ref[0, CONV_HALO + rows:, :] = jnp.where(j < tiles_per_seq - 1, next_ref[0], zeros)
    shifted_rows = rows + 2 * CONV_HALO - SUBLANES
    for b in range(1, SUBLANES):
        sh_ref[b, 0:shifted_rows, :] = sh_ref[0, b:b + shifted_rows, :]

    slab_rows = CONV_ROWS + 2 * CONV_HALO - SUBLANES
    acc = None
    for r0 in range(0, rows, CONV_ROWS):
        if acc is None:
            acc = jnp.zeros((CONV_ROWS // SUBLANES, SUBLANES, lanes), F32)
        else:
            bits = lax.bitcast_convert_type(acc, jnp.uint32)
            acc = lax.shift_right_logical(lax.shift_right_logical(bits, jnp.uint32(16)), jnp.uint32(16)).astype(F32)
        for shift in range(SUBLANES):
            slab = sh_ref[shift, r0:r0 + slab_rows, :].reshape(slab_rows // SUBLANES, SUBLANES, lanes)
            for tap in range(CONV_KERNEL):
                offset = tap + (CONV_HALO - CONV_PAD)
                if offset % SUBLANES == shift:
                    first = offset // SUBLANES
                    acc = acc + slab[first:first + CONV_ROWS // SUBLANES] * taps_ref[tap][None]
        y_ref[0, r0:r0 + CONV_ROWS, :] = acc.reshape(CONV_ROWS, lanes)


def _qkv_conv_proj(h, w_in, layer, n_cols, u, conv_w):
    bsz, s, d = h.shape
    width = u.shape[-1]
    m = bsz * s
    tm = _tile(s, 1024)
    n_groups = width // CONV_LANES
    tn = n_cols // n_groups
    assert tn % LANES == 0 and width % CONV_LANES == 0
    tiles_per_seq = s // tm
    halo_blocks = tm // CONV_HALO
    last_halo = s // CONV_HALO - 1

    def seq_tile(i):
        return i // tiles_per_seq, i % tiles_per_seq

    def prev_map(n, i):
        b, j = seq_tile(i)
        return (b, jnp.maximum(j * halo_blocks - 1, 0), n)

    def cur_map(n, i):
        b, j = seq_tile(i)
        return (b, j, n)

    def next_map(n, i):
        b, j = seq_tile(i)
        return (b, jnp.minimum((j + 1) * halo_blocks, last_halo), n)

    taps = jnp.broadcast_to(conv_w[:, None, :], (CONV_KERNEL, SUBLANES, width))
    shifted = (SUBLANES, tm + 2 * CONV_HALO, CONV_LANES)
    blocks = (_nbytes((tm, d), BF16) + _nbytes((d, tn), BF16) + _nbytes((tm, tn), F32)
              + 2 * _nbytes((CONV_HALO, CONV_LANES), F32) + 2 * _nbytes((tm, CONV_LANES), F32)
              + _nbytes((CONV_KERNEL, SUBLANES, CONV_LANES), F32))
    qkv, y = pl.pallas_call(
        functools.partial(_qkv_conv_kernel, tiles_per_seq=tiles_per_seq),
        grid=(n_groups, m // tm),
        in_specs=[
            pl.BlockSpec((tm, d), lambda n, i: (i, 0)),
            pl.BlockSpec((None, d, tn), lambda n, i: (layer, 0, n)),
            pl.BlockSpec((1, CONV_HALO, CONV_LANES), prev_map),
            pl.BlockSpec((1, tm, CONV_LANES), cur_map),
            pl.BlockSpec((1, CONV_HALO, CONV_LANES), next_map),
            pl.BlockSpec((CONV_KERNEL, SUBLANES, CONV_LANES), lambda n, i: (0, 0, n)),
        ],
        out_specs=[pl.BlockSpec((tm, tn), lambda n, i: (i, n)),
                   pl.BlockSpec((1, tm, CONV_LANES), cur_map)],
        out_shape=[jax.ShapeDtypeStruct((m, n_cols), F32), jax.ShapeDtypeStruct((bsz, s, width), F32)],
        scratch_shapes=[pltpu.VMEM(shifted, F32)],
        compiler_params=_params(("parallel", "parallel"),
                                _vmem_limit(blocks, scratch=_nbytes(shifted, F32), temps=2 * _nbytes((tm, tn), F32))),
        name="qkv_conv_proj",
    )(h.reshape(m, d), w_in, u, u, u, taps)
    return qkv.reshape(bsz, s, n_cols), y


def _glu_kernel(h_ref, wv_ref, wg_ref, *rest):
    n_cast = len(rest) // 2
    o_ref = rest[n_cast]
    h = h_ref[...]
    o_ref[...] = _dot(h, wv_ref[...]) * jax.nn.sigmoid(_dot(h, wg_ref[...]))
    for src, dst in zip(rest[:n_cast], rest[n_cast + 1:]):
        dst[...] = src[...].astype(dst.dtype)


def _glu_proj(h2d, w_in, layer, val_col, gate_col, width, cast=()):
    m, d = h2d.shape
    tm, tn = _tile(m, 1024), _tile(width, 512)
    v_blk, g_blk = val_col // tn, gate_col // tn
    m_tiles = m // tm
    steps = (width // tn) * m_tiles
    blocks = _nbytes((tm, d), BF16) + 2 * _nbytes((d, tn), BF16) + _nbytes((tm, tn), F32)
    in_specs = [
        pl.BlockSpec((tm, d), lambda n, i: (i, 0)),
        pl.BlockSpec((None, d, tn), lambda n, i: (layer, 0, v_blk + n)),
        pl.BlockSpec((None, d, tn), lambda n, i: (layer, 0, g_blk + n)),
    ]
    out_specs = [pl.BlockSpec((tm, tn), lambda n, i: (i, n))]
    out_shape = [jax.ShapeDtypeStruct((m, width), F32)]
    for w in cast:
        depth, rows, cols = w.shape
        per_layer = steps // depth
        chunk = rows // per_layer
        assert steps % depth == 0 and rows % per_layer == 0 and chunk % 16 == 0

        def chunk_map(n, i, per_layer=per_layer):
            step = n * m_tiles + i
            return (step // per_layer, step % per_layer, 0)

        in_specs.append(pl.BlockSpec((None, chunk, cols), chunk_map))
        out_specs.append(pl.BlockSpec((None, chunk, cols), chunk_map))
        out_shape.append(jax.ShapeDtypeStruct(w.shape, BF16))
        blocks += _nbytes((chunk, cols), F32) + _nbytes((chunk, cols), BF16)
    return pl.pallas_call(
        _glu_kernel,
        grid=(width // tn, m_tiles),
        in_specs=in_specs,
        out_specs=out_specs,
        out_shape=out_shape,
        compiler_params=_params(("parallel", "parallel"), _vmem_limit(blocks, temps=4 * _nbytes((tm, tn), F32))),
        name="glu_proj",
    )(h2d, w_in, w_in, *cast)


def _bias_tiles(n_heads):
    slopes = 2.0 ** (-8.0 * jnp.arange(1, n_heads + 1, dtype=F32) / n_heads)
    q = jnp.arange(Q_BLOCK)[:, None]
    c = jnp.arange(K_WINDOW)[None, :]
    branches = []
    for window, dilation in DILATED_BRANCHES:
        assert window == 2 * dilation * HALF_WINDOW
        tiles = []
        for key_start_minus_q_start in (0, -HALF_WINDOW, -2 * HALF_WINDOW):
            rel = key_start_minus_q_start + c - q
            dist = (dilation * jnp.abs(rel)).astype(F32)
            bias = -slopes[:, None, None] * dist[None]
            tiles.append(jnp.where((jnp.abs(rel) <= HALF_WINDOW)[None], bias, NEG_INF))
        branches.append(jnp.stack(tiles, axis=0))
    b = jnp.stack(branches, axis=0)
    b = b.reshape(len(DILATED_BRANCHES), 3, n_heads // 2, 2, Q_BLOCK, K_WINDOW)
    return b.transpose(2, 0, 1, 3, 4, 5)


def _attn_kernel(q_ref, k_ref, v_ref, bias_ref, o_ref, cq_ref, ck_ref, cv_ref, qs_ref, ks_ref, vs_ref,
                 acc_ref, m_ref, l_ref, stage_ref, *, seq):
    lane = lax.broadcasted_iota(jnp.int32, (1, LANES), 1)
    first_head = lane < HEAD_DIM
    _, (_, mid), (_, far) = DILATED_BRANCHES
    sub = far // mid
    len_mid, len_far = seq // mid, seq // far
    gather_mid_rows = min(ATTN_GATHER_ROWS, len_mid)
    gather_far_rows = min(ATTN_GATHER_ROWS, len_far)

    def window(i, length):
        q0 = pl.multiple_of(i * Q_BLOCK, Q_BLOCK)
        k0 = pl.multiple_of(jnp.clip(q0 - HALF_WINDOW, 0, length - K_WINDOW), HALF_WINDOW)
        return q0, k0, (q0 - k0) // HALF_WINDOW

    def partial_softmax(q, kw, vw, branch, kind):
        tops, probs = [], []
        for head in range(2):
            mine = first_head if head == 0 else jnp.logical_not(first_head)
            qh = jnp.where(mine, q, jnp.zeros_like(q))
            s = lax.dot_general(qh, kw, (((1,), (1,)), ((), ())), preferred_element_type=F32)
            s = s + bias_ref[0, branch, kind, head]
            m = jnp.max(s, axis=-1, keepdims=True)
            tops.append(m)
            probs.append(jnp.exp(s - m).astype(BF16))
        zero = jnp.zeros_like(vw)
        ind = jnp.where(first_head, 1.0, 0.0).astype(BF16)
        rhs = jnp.concatenate(
            [jnp.concatenate([jnp.where(first_head, vw, zero), jnp.broadcast_to(ind, vw.shape)], axis=1),
             jnp.concatenate([jnp.where(first_head, zero, vw), jnp.broadcast_to(1 - ind, vw.shape)], axis=1)],
            axis=0)
        res = _dot(jnp.concatenate(probs, axis=1), rhs)
        return jnp.where(first_head, tops[0], tops[1]), res[:, LANES:], res[:, :LANES]

    def merge(m_old, l_old, o_old, m_blk, l_blk, o_blk):
        m_new = jnp.maximum(m_old, m_blk)
        w_old = jnp.exp(m_old - m_new)
        w_blk = jnp.exp(m_blk - m_new)
        return m_new, w_old * l_old + w_blk * l_blk, w_old * o_old + w_blk * o_blk

    def grouped(n_blocks, block):
        unroll = _tile(n_blocks, ATTN_UNROLL, align=1)

        def group(g, carry):
            for u in range(unroll):
                block(g * unroll + u, u)
            return carry

        lax.fori_loop(0, n_blocks // unroll, group, 0)

    def mid_class(r, carry):
        base = pl.multiple_of(r * len_mid, len_mid)

        def gather_mid(c, carry2):
            c0 = pl.multiple_of(c * gather_mid_rows, gather_mid_rows)
            src = pl.ds(r + mid * c0, gather_mid_rows, stride=mid)
            dst = pl.ds(c0, gather_mid_rows)
            x = q_ref[0, src, :]
            cq_ref[dst, :] = x
            qs_ref[dst, :] = x.astype(BF16)
            x = k_ref[0, src, :]
            ck_ref[dst, :] = x
            ks_ref[dst, :] = x.astype(BF16)
            x = v_ref[0, src, :]
            cv_ref[dst, :] = x
            vs_ref[dst, :] = x.astype(BF16)
            return carry2

        lax.fori_loop(0, len_mid // gather_mid_rows, gather_mid, 0)

        def mid_block(i, slot):
            q0, k0, kind = window(i, len_mid)
            m, l, o = partial_softmax(qs_ref[pl.ds(q0, Q_BLOCK), :], ks_ref[pl.ds(k0, K_WINDOW), :],
                                      vs_ref[pl.ds(k0, K_WINDOW), :], 1, kind)
            rows = pl.ds(base + q0, Q_BLOCK)
            m_ref[rows, :] = m
            l_ref[rows, :] = l
            acc_ref[rows, :] = o

        grouped(len_mid // Q_BLOCK, mid_block)

        for c in range(sub):
            def gather_far(g, carry2, c=c):
                g0 = pl.multiple_of(g * gather_far_rows, gather_far_rows)
                src = pl.ds(c + sub * g0, gather_far_rows, stride=sub)
                dst = pl.ds(c * len_far + g0, gather_far_rows)
                qs_ref[dst, :] = cq_ref[src, :].astype(BF16)
                ks_ref[dst, :] = ck_ref[src, :].astype(BF16)
                vs_ref[dst, :] = cv_ref[src, :].astype(BF16)
                return carry2

            lax.fori_loop(0, len_far // gather_far_rows, gather_far, 0)

        far_blocks = len_far // Q_BLOCK

        def far_block(j, slot):
            c = j // far_blocks
            q0, k0, kind = window(j - c * far_blocks, len_far)
            off = pl.multiple_of(c * len_far, len_far)
            m, l, o = partial_softmax(qs_ref[pl.ds(off + q0, Q_BLOCK), :], ks_ref[pl.ds(off + k0, K_WINDOW), :],
                                      vs_ref[pl.ds(off + k0, K_WINDOW), :], 2, kind)
            rows = pl.ds(base + c + sub * q0, Q_BLOCK, stride=sub)
            m, l, o = merge(m_ref[rows, :], l_ref[rows, :], acc_ref[rows, :], m, l, o)
            m_ref[rows, :] = m
            l_ref[rows, :] = l
            acc_ref[rows, :] = o

        grouped(sub * far_blocks, far_block)
        return carry

    lax.fori_loop(0, mid, mid_class, 0)

    per_class = Q_BLOCK // mid

    def near_block(i, slot):
        q0, k0, kind = window(i, seq)
        m, l, o = partial_softmax(q_ref[0, pl.ds(q0, Q_BLOCK), :].astype(BF16),
                                  k_ref[0, pl.ds(k0, K_WINDOW), :].astype(BF16),
                                  v_ref[0, pl.ds(k0, K_WINDOW), :].astype(BF16), 0, kind)
        j0 = pl.multiple_of(i * per_class, per_class)
        for c in range(mid):
            src = pl.ds(c * len_mid + j0, per_class)
            dst = pl.ds(c, per_class, stride=mid)
            stage_ref[slot, 0, dst, :] = m_ref[src, :]
            stage_ref[slot, 1, dst, :] = l_ref[src, :]
            stage_ref[slot, 2, dst, :] = acc_ref[src, :]
        m, l, o = merge(stage_ref[slot, 0], stage_ref[slot, 1], stage_ref[slot, 2], m, l, o)
        o_ref[0, pl.ds(q0, Q_BLOCK), :] = (o / l).astype(o_ref.dtype)

    grouped(seq // Q_BLOCK, near_block)


def _attention(qkv, bias, attn_width):
    bsz, seq, _ = qkv.shape
    n_pairs = attn_width // LANES
    (_, near), (_, mid), (_, far) = DILATED_BRANCHES
    assert near == 1 and far % mid == 0 and Q_BLOCK % mid == 0
    assert seq % (far * Q_BLOCK) == 0 and seq // far >= K_WINDOW

    def slab(col0):
        return pl.BlockSpec((1, seq, LANES), lambda p, b: (b, 0, col0 + p))

    bias_block = (1,) + bias.shape[1:]
    mid_rows = (seq // mid, LANES)
    stage = (ATTN_UNROLL, 3, Q_BLOCK, LANES)
    blocks = 3 * _nbytes((seq, LANES), F32) + _nbytes(bias_block, F32) + _nbytes((seq, LANES), BF16)
    scratch = (3 * _nbytes(mid_rows, F32) + 3 * _nbytes(mid_rows, BF16) + 3 * _nbytes((seq, LANES), F32)
               + _nbytes(stage, F32))
    return pl.pallas_call(
        functools.partial(_attn_kernel, seq=seq),
        grid=(n_pairs, bsz),
        in_specs=[slab(0), slab(n_pairs), slab(2 * n_pairs),
                  pl.BlockSpec(bias_block, lambda p, b: (p, 0, 0, 0, 0, 0))],
        out_specs=pl.BlockSpec((1, seq, LANES), lambda p, b: (b, 0, p)),
        out_shape=jax.ShapeDtypeStruct((bsz, seq, attn_width), BF16),
        scratch_shapes=([pltpu.VMEM(mid_rows, F32)] * 3 + [pltpu.VMEM(mid_rows, BF16)] * 3
                        + [pltpu.VMEM((seq, LANES), F32)] * 3 + [pltpu.VMEM(stage, F32)]),
        compiler_params=_params(("parallel", "parallel"),
                                _vmem_limit(blocks, scratch=scratch, temps=16 * _nbytes((Q_BLOCK, K_WINDOW), F32))),
        name="attention",
    )(qkv, qkv, qkv, bias)


OUT_SUBTILES = 2


def _out_kernel(at_ref, y_ref, x_ref, w_ref, ag_ref, cb_ref, cg_ref, cbeta_ref, og_ref, gate_ref, lg_ref, lb_ref,
                sc_ref, sh_ref, x_out, h_out, *, alpha):
    sub = at_ref.shape[1] // OUT_SUBTILES
    for r0 in range(0, at_ref.shape[1], sub):
        rows = pl.ds(r0, sub)
        attn = _rms_norm(at_ref[0, rows, :].astype(F32), ag_ref[...]).astype(BF16)
        conv = _layer_norm(y_ref[0, rows, :] + cb_ref[...], cg_ref[...], cbeta_ref[...])
        conv = _rms_norm(conv * jax.nn.sigmoid(conv), og_ref[...]).astype(BF16)
        mix = _dot(jnp.concatenate([attn, conv], axis=-1), w_ref[...])
        x = _layer_norm(alpha * x_ref[0, rows, :] + (1.0 + gate_ref[0]) * mix, lg_ref[...], lb_ref[...])
        x_out[0, rows, :] = x
        h_out[0, rows, :] = (x * (1.0 + sc_ref[0]) + sh_ref[0]).astype(BF16)


def _out_proj(attn, y, conv_b, conv_ln_g, conv_ln_b, conv_out_g, x, w_out, layer, attn_g, gate, ln_g, ln_b, sc, sh,
              alpha):
    bsz, s, d = x.shape
    aw = attn.shape[-1]
    cw = y.shape[-1]
    tm = _tile(s, 512)

    def rows(width):
        return pl.BlockSpec((1, tm, width), lambda b, i: (b, i, 0))

    def vec(width):
        return pl.BlockSpec((1, width), lambda b, i: (0, 0))

    bvec = pl.BlockSpec((1, 1, d), lambda b, i: (b, 0, 0))
    blocks = (_nbytes((tm, aw), BF16) + _nbytes((tm, cw), F32) + 2 * _nbytes((tm, d), F32)
              + _nbytes((tm, d), BF16) + _nbytes((d, d), BF16))
    return pl.pallas_call(
        functools.partial(_out_kernel, alpha=alpha),
        grid=(bsz, s // tm),
        in_specs=[rows(aw), rows(cw), rows(d), pl.BlockSpec((None, d, d), lambda b, i: (layer, 0, 0)), vec(aw),
                  vec(cw), vec(cw), vec(cw), vec(cw), bvec, vec(d), vec(d), bvec, bvec],
        out_specs=[rows(d), rows(d)],
        out_shape=[jax.ShapeDtypeStruct((bsz, s, d), F32), jax.ShapeDtypeStruct((bsz, s, d), BF16)],
        compiler_params=_params(("parallel", "parallel"), _vmem_limit(blocks, temps=4 * _nbytes((tm, d), F32))),
        name="out_proj",
    )(attn, y, x, w_out, attn_g.reshape(1, aw), conv_b.reshape(1, cw), conv_ln_g.reshape(1, cw),
      conv_ln_b.reshape(1, cw), conv_out_g.reshape(1, cw), gate, ln_g.reshape(1, d), ln_b.reshape(1, d), sc, sh)


def _mlp_kernel(h_ref, w1_ref, b1_ref, w2_ref, b2_ref, x_ref, gate_ref, lg_ref, lb_ref, *rest, alpha, modulate):
    if modulate:
        sc_ref, sh_ref, x_out, h_out, acc_ref = rest
    else:
        x_out, acc_ref = rest
    t, f = pl.program_id(0), pl.program_id(1)
    n_tiles = pl.num_programs(0) - 1
    slot = t % 2

    def chunk():
        hid = jnp.maximum(_dot(h_ref[0], w1_ref[...]) + b1_ref[...], 0.0)
        return _dot((hid * hid).astype(BF16), w2_ref[...])

    def epilogue(src):
        ff = acc_ref[src] + b2_ref[...]
        x = _layer_norm(alpha * x_ref[0] + (1.0 + gate_ref[0]) * ff, lg_ref[...], lb_ref[...])
        x_out[0] = x
        if modulate:
            h_out[0] = (x * (1.0 + sc_ref[0]) + sh_ref[0]).astype(BF16)

    @pl.when((t == 0) & (f == 0))
    def _():
        acc_ref[1] = jnp.zeros(acc_ref.shape[1:], F32)

    for cur in range(2):
        mine = slot == cur

        @pl.when(mine & (f == 0) & (t < n_tiles))
        def _(cur=cur):
            acc_ref[cur] = chunk()
            epilogue(1 - cur)

        @pl.when(mine & (f > 0) & (t < n_tiles))
        def _(cur=cur):
            acc_ref[cur] += chunk()

        @pl.when(mine & (f == 0) & (t == n_tiles))
        def _(cur=cur):
            epilogue(1 - cur)


def _mlp(h, w1, b1, w2, b2, layer, x, gate, ln_g, ln_b, alpha, next_mod=None):
    bsz, s, d = x.shape
    ff_dim = w1.shape[-1]
    tm, tf = _tile(s, 512), _tile(ff_dim, 1024)
    per_batch = s // tm
    n_tiles, n_chunks = bsz * per_batch, ff_dim // tf
    modulate = next_mod is not None

    def prev(t):
        return jnp.maximum(t - 1, 0)

    cur_rows = pl.BlockSpec((1, tm, d), lambda t, f: (jnp.minimum(t, n_tiles - 1), 0, 0))
    prev_rows = pl.BlockSpec((1, tm, d), lambda t, f: (prev(t), 0, 0))
    prev_bvec = pl.BlockSpec((1, 1, d), lambda t, f: (prev(t) // per_batch, 0, 0))
    vec_d = pl.BlockSpec((1, d), lambda t, f: (0, 0))

    def chunk_of(t, f):
        return jnp.where(t == n_tiles, n_chunks - 1, f)

    in_specs = [
        cur_rows,
        pl.BlockSpec((None, d, tf), lambda t, f: (layer, 0, chunk_of(t, f))),
        pl.BlockSpec((1, tf), lambda t, f: (0, chunk_of(t, f))),
        pl.BlockSpec((None, tf, d), lambda t, f: (layer, chunk_of(t, f), 0)),
        vec_d, prev_rows, prev_bvec, vec_d, vec_d,
    ]
    tiles = (n_tiles, tm, d)
    args = [h.reshape(tiles), w1, b1.reshape(1, ff_dim), w2, b2.reshape(1, d), x.reshape(tiles), gate,
            ln_g.reshape(1, d), ln_b.reshape(1, d)]
    out_specs = [prev_rows]
    out_shape = [jax.ShapeDtypeStruct(tiles, F32)]
    blocks = (_nbytes((tm, d), BF16) + 2 * _nbytes((d, tf), BF16) + 2 * _nbytes((tm, d), F32))
    if modulate:
        in_specs += [prev_bvec, prev_bvec]
        args += list(next_mod)
        out_specs.append(prev_rows)
        out_shape.append(jax.ShapeDtypeStruct(tiles, BF16))
        blocks += _nbytes((tm, d), BF16)
    acc = (2, tm, d)
    out = pl.pallas_call(
        functools.partial(_mlp_kernel, alpha=alpha, modulate=modulate),
        grid=(n_tiles + 1, n_chunks),
        in_specs=in_specs,
        out_specs=out_specs,
        out_shape=out_shape,
        scratch_shapes=[pltpu.VMEM(acc, F32)],
        compiler_params=_params(("arbitrary", "arbitrary"),
                                _vmem_limit(blocks, scratch=_nbytes(acc, F32), temps=3 * _nbytes((tm, tf), F32))),
        name="mlp",
    )(*args)
    out = [o.reshape(bsz, s, -1) for o in out]
    return out if modulate else (out[0], None)


def _trunk(x, mods, emb_ln_g, emb_ln_b, w_in, conv_w, conv_b, conv_ln_g, conv_ln_b, attn_out_g, conv_out_g,
           w_out, ln1_g, ln1_b, w_mlp1, b_mlp1, w_mlp2, b_mlp2, ln2_g, ln2_b, bias):
    bsz, s, d = x.shape
    depth = w_in.shape[0]
    aw = d // 2
    cw = d - aw
    alpha = (2 * depth) ** 0.25

    def mod(layer, k):
        return mods[layer, :, k * d:(k + 1) * d].reshape(bsz, 1, d)

    x, h = _embed(x, emb_ln_g, emb_ln_b, mod(0, 1), mod(0, 0))
    for layer in range(depth):
        h2d = h.reshape(bsz * s, d)
        cast = tuple(w for w in (w_mlp1, w_mlp2) if w.dtype != BF16)
        u, *converted = _glu_proj(h2d, w_in, layer, 3 * aw, 3 * aw + cw, cw, cast)
        if converted:
            w_mlp1, w_mlp2 = converted
        u = u.reshape(bsz, s, cw)
        qkv, y = _qkv_conv_proj(h, w_in, layer, 3 * aw, u, conv_w[layer])
        attn = _attention(qkv, bias, aw)
        x, h = _out_proj(attn, y, conv_b[layer], conv_ln_g[layer], conv_ln_b[layer], conv_out_g[layer], x, w_out, layer, attn_out_g[layer], mod(layer, 2), ln1_g[layer],
                         ln1_b[layer], mod(layer, 4), mod(layer, 3), alpha)
        next_mod = (mod(layer + 1, 1), mod(layer + 1, 0)) if layer + 1 < depth else None
        x, h = _mlp(h, w_mlp1, b_mlp1[layer], w_mlp2, b_mlp2[layer], layer, x, mod(layer, 5),
                    ln2_g[layer], ln2_b[layer], alpha, next_mod)
    return x, w_mlp1, w_mlp2


def kernel(x_prompt, x_sample, c_prompt, c_sample, emb_ln_g, emb_ln_b, w_ada, b_ada, w_in, conv_w, conv_b,
           conv_ln_g, conv_ln_b, attn_out_g, conv_out_g, w_out, ln1_g, ln1_b, w_mlp1, b_mlp1, w_mlp2, b_mlp2,
           ln2_g, ln2_b):
    d = x_prompt.shape[-1]
    n_prompt, n_sample = c_prompt.shape[0], c_sample.shape[0]
    pad = (-(n_prompt + n_sample)) % 8
    c_all = jnp.concatenate([c_prompt, c_sample, jnp.zeros((pad, d), F32)], axis=0)
    mods = _ada_mod(c_all, w_ada, b_ada)
    n_heads = (d // 2) // HEAD_DIM
    bias = _bias_tiles(n_heads)
    q_scale = jnp.where(jnp.arange(w_in.shape[-1]) < d // 2, HEAD_DIM ** -0.5, 1.0).astype(F32)
    mixer = (emb_ln_g, emb_ln_b, (w_in * q_scale).astype(BF16), conv_w, conv_b, conv_ln_g, conv_ln_b, attn_out_g,
             conv_out_g, w_out.astype(BF16), ln1_g, ln1_b)

    def run(x, trunk_mods, w1, w2):
        return _trunk(x, trunk_mods, *mixer, w1, b_mlp1, w2, b_mlp2, ln2_g, ln2_b, bias)

    y_prompt, w1_bf16, w2_bf16 = run(x_prompt, mods[:, :n_prompt], w_mlp1, w_mlp2)
    y_sample, _, _ = run(x_sample, mods[:, n_prompt:n_prompt + n_sample], w1_bf16, w2_bf16)
    return (y_prompt, y_sample)
```
